```python
import math
import jax, jax.numpy as jnp
from jax import lax
import numpy as np

D_MODEL = 1024
BATCH = 8
SEQ = 2048
DEPTH = 4
DEC_BATCH = 32
DEC_SEQ = 1
PAST_LEN = 8192
PAGE_SIZE = 128

N_EVEN = (DEPTH + 1) // 2
N_ODD = DEPTH // 2
MIX_WIDTH = D_MODEL
GROUP_WIDTH = MIX_WIDTH // 2
S5_CH = 16
S5_GROUPS = GROUP_WIDTH // S5_CH
S5_STATE = 64
DH = 64
N_DIFF_HEADS = GROUP_WIDTH // (2 * DH)
LRU_WIDTH = GROUP_WIDTH
LRU_BLOCKS = 8
LRU_BLOCK = LRU_WIDTH // LRU_BLOCKS
LRU_C = 8.0
CONV_W = 4
N_FOX_HEADS = GROUP_WIDTH // DH
D_FF = ((8 * D_MODEL // 3 + 127) // 128) * 128
Q_BLOCK = 128
EPS = 1e-6
NEG = -1e30

kernel_name = 'hybrid_s5_diffattn_rglru_fox_decode_step'


def rms_norm(x, g):
    xf = x.astype(jnp.float32)
    y = xf * lax.rsqrt(jnp.mean(xf * xf, axis=-1, keepdims=True) + EPS)
    return (y * g.astype(jnp.float32)).astype(x.dtype)


def swiglu(x, w_gate, w_up, w_down):
    return (jax.nn.silu(x @ w_gate) * (x @ w_up)) @ w_down


def gather_pages(cache, page_table):
    g = cache[page_table]
    return g.reshape(page_table.shape[0], -1, *cache.shape[2:])


def sweep_queries(block_fn, q_args, q_pos):
    tq = q_pos.shape[0]
    if tq > Q_BLOCK and tq % Q_BLOCK == 0:
        nb = tq // Q_BLOCK
        def split(a):
            return jnp.moveaxis(a.reshape(a.shape[0], nb, Q_BLOCK, *a.shape[2:]), 1, 0)
        blocks = tuple(split(a) for a in q_args) + (q_pos.reshape(nb, Q_BLOCK),)
        out = lax.map(lambda a: block_fn(*a), blocks)
        out = jnp.moveaxis(out, 0, 1)
        return out.reshape(out.shape[0], tq, *out.shape[3:])
    return block_fn(*q_args, q_pos)


def cmul(ar, ai, br, bi):
    return ar * br - ai * bi, ar * bi + ai * br


def _s5_combine(e1, e2):
    a1r, a1i, b1r, b1i = e1
    a2r, a2i, b2r, b2i = e2
    ar, ai = cmul(a2r, a2i, a1r, a1i)
    br, bi = cmul(a2r, a2i, b1r, b1i)
    return ar, ai, br + b2r, bi + b2i


def _lin_combine(e1, e2):
    a1, b1 = e1
    a2, b2 = e2
    return a1 * a2, a2 * b1 + b2


def s5_mixer(u, h0_re, h0_im, a_re, a_im, b_re, b_im, c_re, c_im, d_skip, log_dt, w_glu):
    bt, t, _ = u.shape
    uf = u.astype(jnp.float32).reshape(bt, t, S5_GROUPS, S5_CH)
    dt = jnp.exp(log_dt.astype(jnp.float32))[:, None]
    lr, li = a_re.astype(jnp.float32), a_im.astype(jnp.float32)
    mag = jnp.exp(lr * dt)
    ab_re, ab_im = mag * jnp.cos(li * dt), mag * jnp.sin(li * dt)
    den = lr * lr + li * li
    nr, ni = ab_re - 1.0, ab_im
    co_re, co_im = (nr * lr + ni * li) / den, (ni * lr - nr * li) / den
    bu_re = jnp.einsum('gph,btgh->btgp', b_re.astype(jnp.float32), uf)
    bu_im = jnp.einsum('gph,btgh->btgp', b_im.astype(jnp.float32), uf)
    bb_re, bb_im = cmul(co_re, co_im, bu_re, bu_im)
    h0r, h0i = cmul(ab_re, ab_im, h0_re.astype(jnp.float32), h0_im.astype(jnp.float32))
    bb_re = bb_re.at[:, 0].add(h0r)
    bb_im = bb_im.at[:, 0].add(h0i)
    a_full_re = jnp.broadcast_to(ab_re, bb_re.shape)
    a_full_im = jnp.broadcast_to(ab_im, bb_im.shape)
    _, _, s_re, s_im = lax.associative_scan(_s5_combine, (a_full_re, a_full_im, bb_re, bb_im), axis=1)
    y = (jnp.einsum('ghp,btgp->btgh', c_re.astype(jnp.float32), s_re)
         - jnp.einsum('ghp,btgp->btgh', c_im.astype(jnp.float32), s_im))
    y = y.reshape(bt, t, GROUP_WIDTH) + d_skip.astype(jnp.float32) * u.astype(jnp.float32)
    g = jax.nn.gelu(y).astype(u.dtype)
    out = g * jax.nn.sigmoid(g @ w_glu)
    return out, s_re[:, -1], s_im[:, -1]


def diff_attn_block(q, q_pos, k, v, k_pos, lam):
    s = jnp.einsum('bqhcd,bkhcd->bhcqk', q, k).astype(jnp.float32) * (DH ** -0.5)
    s = jnp.where(k_pos[None, :] <= q_pos[:, None], s, NEG)
    p = jax.nn.softmax(s, axis=-1)
    w = p[:, :, 0] - lam * p[:, :, 1]
    return jnp.einsum('bhqk,bkhe->bqhe', w.astype(v.dtype), v)


def fox_block(q, fq, q_pos, k, v, cum_k, k_pos):
    s = jnp.einsum('bqhd,bkhd->bhqk', q, k).astype(jnp.float32) * (DH ** -0.5)
    s = s + (jnp.transpose(fq, (0, 2, 1))[..., :, None] - cum_k[:, :, None, :])
    s = jnp.where(k_pos[None, :] <= q_pos[:, None], s, NEG)
    p = jax.nn.softmax(s, axis=-1)
    return jnp.einsum('bhqk,bkhd->bqhd', p.astype(v.dtype), v)


def rglru(xb, gate, conv_buf, h0, conv_w, conv_b, w_a, b_a, w_x, b_x, lam):
    bt, t, _ = xb.shape
    xp = jnp.concatenate([conv_buf.astype(xb.dtype), xb], axis=1)
    xc = conv_b + sum(conv_w[j] * xp[:, j:j + t] for j in range(CONV_W))
    new_buf = xp[:, t:]
    xblk = xc.reshape(bt, t, LRU_BLOCKS, LRU_BLOCK)
    r = jax.nn.sigmoid(jnp.einsum('btnd,nde->btne', xblk, w_a).reshape(bt, t, LRU_WIDTH) + b_a)
    i = jax.nn.sigmoid(jnp.einsum('btnd,nde->btne', xblk, w_x).reshape(bt, t, LRU_WIDTH) + b_x)
    log_a = (-LRU_C * jax.nn.softplus(-lam.astype(jnp.float32))) * r.astype(jnp.float32)
    a = jnp.exp(log_a)
    b = jnp.sqrt(-jnp.expm1(2.0 * log_a)) * (i * xc).astype(jnp.float32)
    b = b.at[:, 0].add(a[:, 0] * h0.astype(jnp.float32))
    _, h = lax.associative_scan(_lin_combine, (a, b), axis=1)
    y = jax.nn.gelu(gate) * h.astype(xb.dtype)
    return y, new_buf, h[:, -1]


def even_mixer(hn, past_k, past_v, h0_re, h0_im, w_in, w_out,
               a_re, a_im, b_re, b_im, c_re, c_im, d_skip, log_dt, w_glu,
               lq1, lk1, lq2, lk2, subln_g, lam_init):
    bt, t, _ = hn.shape
    u, q, k, v = jnp.split(hn @ w_in, 4, axis=-1)
    q = q.reshape(bt, t, N_DIFF_HEADS, 2, DH)
    k_rows = k.reshape(bt, t, 2 * N_DIFF_HEADS, DH)
    v_rows = v.reshape(bt, t, N_DIFF_HEADS, 2 * DH)
    if past_k is None:
        k_all, v_all = k_rows, v_rows
    else:
        k_all = jnp.concatenate([past_k.astype(k_rows.dtype), k_rows], axis=1)
        v_all = jnp.concatenate([past_v.astype(v_rows.dtype), v_rows], axis=1)
    tk = k_all.shape[1]
    k_pos = jnp.arange(tk)
    q_pos = tk - t + jnp.arange(t)
    k_all5 = k_all.reshape(bt, tk, N_DIFF_HEADS, 2, DH)
    lam = (jnp.exp(jnp.sum(lq1.astype(jnp.float32) * lk1.astype(jnp.float32)))
           - jnp.exp(jnp.sum(lq2.astype(jnp.float32) * lk2.astype(jnp.float32))) + lam_init)
    attn = sweep_queries(lambda qb, qp: diff_attn_block(qb, qp, k_all5, v_all, k_pos, lam), (q,), q_pos)
    attn = (rms_norm(attn, subln_g) * (1.0 - lam_init)).reshape(bt, t, GROUP_WIDTH)
    s5_out, hr, hi = s5_mixer(u, h0_re, h0_im, a_re, a_im, b_re, b_im, c_re, c_im, d_skip, log_dt, w_glu)
    mix = jnp.concatenate([s5_out, attn.astype(s5_out.dtype)], axis=-1) @ w_out
    return mix, k_rows, v_rows, hr, hi


def odd_mixer(hn, past_k, past_v, past_logf, conv_buf, h0, w_in, w_out, b_f,
              conv_w, conv_b, w_a, b_a, w_x, b_x, lam):
    bt, t, _ = hn.shape
    w = LRU_WIDTH
    cuts = [w, 2 * w, 2 * w + GROUP_WIDTH, 2 * w + 2 * GROUP_WIDTH, 2 * w + 3 * GROUP_WIDTH]
    xb, gate, q, k, v, f = jnp.split(hn @ w_in, cuts, axis=-1)
    q = q.reshape(bt, t, N_FOX_HEADS, DH)
    k_rows = k.reshape(bt, t, N_FOX_HEADS, DH)
    v_rows = v.reshape(bt, t, N_FOX_HEADS, DH)
    logf = jax.nn.log_sigmoid((f + b_f).astype(jnp.float32))
    if past_k is None:
        k_all, v_all, logf_all = k_rows, v_rows, logf
    else:
        k_all = jnp.concatenate([past_k.astype(k_rows.dtype), k_rows], axis=1)
        v_all = jnp.concatenate([past_v.astype(v_rows.dtype), v_rows], axis=1)
        logf_all = jnp.concatenate([past_logf.astype(jnp.float32), logf], axis=1)
    tk = k_all.shape[1]
    k_pos = jnp.arange(tk)
    q_pos = tk - t + jnp.arange(t)
    cum = jnp.cumsum(logf_all, axis=1)
    cum_k = jnp.transpose(cum, (0, 2, 1))
    attn = sweep_queries(lambda qb, fqb, qp: fox_block(qb, fqb, qp, k_all, v_all, cum_k, k_pos),
                         (q, cum[:, tk - t:]), q_pos)
    attn = attn.reshape(bt, t, GROUP_WIDTH)
    lru_out, new_buf, h_last = rglru(xb, gate, conv_buf, h0, conv_w, conv_b, w_a, b_a, w_x, b_x, lam)
    mix = jnp.concatenate([lru_out, attn.astype(lru_out.dtype)], axis=-1) @ w_out
    return mix, k_rows, v_rows, logf, new_buf, h_last


def setup_inputs(seed: int = 0) -> dict:
    key = jax.random.key(seed)
    ks = iter(jax.random.split(key, 64))

    def nrm(shape, scale):
        return jax.random.normal(next(ks), shape, jnp.float32) * scale

    n_pages = PAST_LEN // PAGE_SIZE
    n_phys = (5 * DEC_BATCH * n_pages + 3) // 4
    W = GROUP_WIDTH
    H = N_FOX_HEADS

    x_prompt = nrm((BATCH, SEQ, D_MODEL), 1.0)
    x_sample = nrm((DEC_BATCH, DEC_SEQ, D_MODEL), 1.0)
    cache_k_diff = nrm((N_EVEN, n_phys, PAGE_SIZE, 2 * N_DIFF_HEADS, DH), 1.0)
    cache_v_diff = nrm((N_EVEN, n_phys, PAGE_SIZE, N_DIFF_HEADS, 2 * DH), 1.0)
    state_s5_re = nrm((N_EVEN, DEC_BATCH, S5_GROUPS, S5_STATE), 0.3)
    state_s5_im = nrm((N_EVEN, DEC_BATCH, S5_GROUPS, S5_STATE), 0.3)
    cache_k_fox = nrm((N_ODD, n_phys, PAGE_SIZE, H, DH), 1.0)
    cache_v_fox = nrm((N_ODD, n_phys, PAGE_SIZE, H, DH), 1.0)
    cache_logf_fox = jax.nn.log_sigmoid(4.0 + nrm((N_ODD, n_phys, PAGE_SIZE, H), 1.0))
    state_conv = nrm((N_ODD, DEC_BATCH, CONV_W - 1, LRU_WIDTH), 1.0)
    state_lru = nrm((N_ODD, DEC_BATCH, LRU_WIDTH), 0.5)
    perm = jax.random.permutation(next(ks), n_phys)
    page_table = perm[:DEC_BATCH * n_pages].reshape(DEC_BATCH, n_pages).astype(jnp.int32)

    norm_g = 1.0 + nrm((DEPTH, 3, D_MODEL), 0.01)
    final_norm_g = 1.0 + nrm((D_MODEL,), 0.01)
    w_ffn_gate = nrm((DEPTH, 2, D_MODEL, D_FF), D_MODEL ** -0.5)
    w_ffn_up = nrm((DEPTH, 2, D_MODEL, D_FF), D_MODEL ** -0.5)
    w_ffn_down = nrm((DEPTH, 2, D_FF, D_MODEL), D_FF ** -0.5)

    w_in_even = nrm((N_EVEN, D_MODEL, 4 * W), D_MODEL ** -0.5)
    w_out_even = nrm((N_EVEN, MIX_WIDTH, D_MODEL), MIX_WIDTH ** -0.5)
    s5_a_re = -0.5 + nrm((N_EVEN, S5_GROUPS, S5_STATE), 0.01)
    s5_a_im = (jnp.pi * jnp.arange(S5_STATE, dtype=jnp.float32)) + nrm((N_EVEN, S5_GROUPS, S5_STATE), 0.01)
    s5_b_re = nrm((N_EVEN, S5_GROUPS, S5_STATE, S5_CH), (2 * S5_CH) ** -0.5)
    s5_b_im = nrm((N_EVEN, S5_GROUPS, S5_STATE, S5_CH), (2 * S5_CH) ** -0.5)
    s5_c_re = nrm((N_EVEN, S5_GROUPS, S5_CH, S5_STATE), (2 * S5_STATE) ** -0.5)
    s5_c_im = nrm((N_EVEN, S5_GROUPS, S5_CH, S5_STATE), (2 * S5_STATE) ** -0.5)
    s5_d = nrm((N_EVEN, W), 1.0)
    s5_log_dt = jax.random.uniform(next(ks), (N_EVEN, S5_GROUPS), jnp.float32,
                                   minval=math.log(1e-3), maxval=math.log(1e-1))
    s5_w_glu = nrm((N_EVEN, W, W), W ** -0.5)
    lambda_q1 = nrm((N_EVEN, DH), 0.1)
    lambda_k1 = nrm((N_EVEN, DH), 0.1)
    lambda_q2 = nrm((N_EVEN, DH), 0.1)
    lambda_k2 = nrm((N_EVEN, DH), 0.1)
    diff_subln_g = 1.0 + nrm((N_EVEN, 2 * DH), 0.01)

    w_in_odd = nrm((N_ODD, D_MODEL, 2 * LRU_WIDTH + 3 * W + H), D_MODEL ** -0.5)
    w_out_odd = nrm((N_ODD, MIX_WIDTH, D_MODEL), MIX_WIDTH ** -0.5)
    fox_b_f = jax.random.uniform(next(ks), (N_ODD, H), jnp.float32, minval=2.0, maxval=6.0)
    conv_w = nrm((N_ODD, CONV_W, LRU_WIDTH), CONV_W ** -0.5)
    conv_b = nrm((N_ODD, LRU_WIDTH), 0.01)
    lru_w_a = nrm((N_ODD, LRU_BLOCKS, LRU_BLOCK, LRU_BLOCK), LRU_BLOCK ** -0.5)
    lru_b_a = nrm((N_ODD, LRU_WIDTH), 0.01)
    lru_w_x = nrm((N_ODD, LRU_BLOCKS, LRU_BLOCK, LRU_BLOCK), LRU_BLOCK ** -0.5)
    lru_b_x = nrm((N_ODD, LRU_WIDTH), 0.01)
    a_pow = jax.random.uniform(next(ks), (N_ODD, LRU_WIDTH), jnp.float32, minval=0.9, maxval=0.999)
    s = a_pow ** (1.0 / LRU_C)
    lru_lambda = jnp.log(s) - jnp.log1p(-s)

    return {'x_prompt': x_prompt, 'x_sample': x_sample,
            'cache_k_diff': cache_k_diff, 'cache_v_diff': cache_v_diff,
            'state_s5_re': state_s5_re, 'state_s5_im': state_s5_im,
            'cache_k_fox': cache_k_fox, 'cache_v_fox': cache_v_fox, 'cache_logf_fox': cache_logf_fox,
            'state_conv': state_conv, 'state_lru': state_lru, 'page_table': page_table,
            'norm_g': norm_g, 'final_norm_g': final_norm_g,
            'w_ffn_gate': w_ffn_gate, 'w_ffn_up': w_ffn_up, 'w_ffn_down': w_ffn_down,
            'w_in_even': w_in_even, 'w_out_even': w_out_even,
            's5_a_re': s5_a_re, 's5_a_im': s5_a_im, 's5_b_re': s5_b_re, 's5_b_im': s5_b_im,
            's5_c_re': s5_c_re, 's5_c_im': s5_c_im, 's5_d': s5_d, 's5_log_dt': s5_log_dt,
            's5_w_glu': s5_w_glu, 'lambda_q1': lambda_q1, 'lambda_k1': lambda_k1,
            'lambda_q2': lambda_q2, 'lambda_k2': lambda_k2, 'diff_subln_g': diff_subln_g,
            'w_in_odd': w_in_odd, 'w_out_odd': w_out_odd, 'fox_b_f': fox_b_f,
            'conv_w': conv_w, 'conv_b': conv_b, 'lru_w_a': lru_w_a, 'lru_b_a': lru_b_a,
            'lru_w_x': lru_w_x, 'lru_b_x': lru_b_x, 'lru_lambda': lru_lambda}


def reference(x_prompt, x_sample, cache_k_diff, cache_v_diff, state_s5_re, state_s5_im,
              cache_k_fox, cache_v_fox, cache_logf_fox, state_conv, state_lru, page_table,
              norm_g, final_norm_g, w_ffn_gate, w_ffn_up, w_ffn_down,
              w_in_even, w_out_even, s5_a_re, s5_a_im, s5_b_re, s5_b_im, s5_c_re, s5_c_im,
              s5_d, s5_log_dt, s5_w_glu, lambda_q1, lambda_k1, lambda_q2, lambda_k2, diff_subln_g,
              w_in_odd, w_out_odd, fox_b_f, conv_w, conv_b, lru_w_a, lru_b_a, lru_w_x, lru_b_x,
              lru_lambda):

    def trunk(x, sample):
        bt = x.shape[0]
        kd, vd, sr, si, kf, vf, lf, cv, lr = ([] for _ in range(9))
        for l in range(DEPTH):
            g = norm_g[l]
            x = x + 0.5 * swiglu(rms_norm(x, g[0]), w_ffn_gate[l, 0], w_ffn_up[l, 0], w_ffn_down[l, 0])
            hn = rms_norm(x, g[1])
            if l % 2 == 0:
                e = l // 2
                if sample:
                    past_k = gather_pages(cache_k_diff[e], page_table)
                    past_v = gather_pages(cache_v_diff[e], page_table)
                    h0r, h0i = state_s5_re[e], state_s5_im[e]
                else:
                    past_k = past_v = None
                    h0r = jnp.zeros((bt, S5_GROUPS, S5_STATE), jnp.float32)
                    h0i = jnp.zeros((bt, S5_GROUPS, S5_STATE), jnp.float32)
                mix, k_new, v_new, hr, hi = even_mixer(
                    hn, past_k, past_v, h0r, h0i, w_in_even[e], w_out_even[e],
                    s5_a_re[e], s5_a_im[e], s5_b_re[e], s5_b_im[e], s5_c_re[e], s5_c_im[e],
                    s5_d[e], s5_log_dt[e], s5_w_glu[e],
                    lambda_q1[e], lambda_k1[e], lambda_q2[e], lambda_k2[e], diff_subln_g[e],
                    0.8 - 0.6 * math.exp(-0.3 * l))
                kd.append(k_new); vd.append(v_new); sr.append(hr); si.append(hi)
            else:
                o = l // 2
                if sample:
                    past_k = gather_pages(cache_k_fox[o], page_table)
                    past_v = gather_pages(cache_v_fox[o], page_table)
                    past_logf = gather_pages(cache_logf_fox[o], page_table)
                    buf, h0 = state_conv[o], state_lru[o]
                else:
                    past_k = past_v = past_logf = None
                    buf = jnp.zeros((bt, CONV_W - 1, LRU_WIDTH), hn.dtype)
                    h0 = jnp.zeros((bt, LRU_WIDTH), jnp.float32)
                mix, k_new, v_new, logf_new, buf_new, h_new = odd_mixer(
                    hn, past_k, past_v, past_logf, buf, h0, w_in_odd[o], w_out_odd[o], fox_b_f[o],
                    conv_w[o], conv_b[o], lru_w_a[o], lru_b_a[o], lru_w_x[o], lru_b_x[o], lru_lambda[o])
                kf.append(k_new); vf.append(v_new); lf.append(logf_new); cv.append(buf_new); lr.append(h_new)
            x = x + mix
            x = x + 0.5 * swiglu(rms_norm(x, g[2]), w_ffn_gate[l, 1], w_ffn_up[l, 1], w_ffn_down[l, 1])
        y = rms_norm(x, final_norm_g)
        return (y, jnp.stack(kd), jnp.stack(vd), jnp.stack(sr), jnp.stack(si),
                jnp.stack(kf), jnp.stack(vf), jnp.stack(lf), jnp.stack(cv), jnp.stack(lr))

    (y_prompt, k_diff_p, v_diff_p, s5_re_p, s5_im_p,
     k_fox_p, v_fox_p, logf_p, conv_p, lru_p) = trunk(x_prompt, False)
    (y_sample, k_diff_s, v_diff_s, s5_re_s, s5_im_s,
     k_fox_s, v_fox_s, logf_s, conv_s, lru_s) = trunk(x_sample, True)
    return (y_prompt, y_sample,
            k_diff_p, v_diff_p, s5_re_p, s5_im_p, k_fox_p, v_fox_p, logf_p, conv_p, lru_p,
            k_diff_s, v_diff_s, s5_re_s, s5_im_s, k_fox_s, v_fox_s, logf_s, conv_s, lru_s)
```

```python
import functools
import math

import jax
import jax.numpy as jnp
from jax import lax
from jax.experimental import pallas as pl
from jax.experimental.pallas import tpu as pltpu

F32 = jnp.float32
BF = jnp.bfloat16

D_MODEL = 1024
GROUP_WIDTH = 512
DH = 64
S5_GROUPS = 32
S5_STATE = 64
S5_CH = 16
S5_LANES = S5_GROUPS * S5_STATE
S5_BLOCKS = 4
LRU_BLOCKS = 8
LRU_C = 8.0
CONV_W = 4
N_FOX_HEADS = 8
FF_CHUNK = 256
EPS = 1e-6
NEG = -1e30
ATTN_SCALE = DH ** -0.5

VMEM_LIMIT_V7X = 56 * 1024 * 1024
PAGES_PER_STEP = 8


def _cp(n_axes, vmem=VMEM_LIMIT_V7X):
    return pltpu.CompilerParams(dimension_semantics=("arbitrary",) * n_axes,
                                vmem_limit_bytes=vmem)


def _rms(x, g):
    return x * lax.rsqrt(jnp.mean(x * x, axis=-1, keepdims=True) + EPS) * g


def _full(shape, single=True):
    nd = len(shape)
    kw = {"pipeline_mode": pl.Buffered(1)} if single else {}
    return pl.BlockSpec(shape, lambda *_: (0,) * nd, **kw)


def _row_tile(n, want):
    t = min(n, want)
    assert n % t == 0
    return t


def _ffn_body(x_ref, g_ref, wg_ref, wu_ref, wd_ref, fg_ref, o_ref, hn_ref, *, n_chunks, final):
    x = x_ref[...]
    hn_ref[...] = _rms(x, g_ref[...]).astype(BF)
    o_ref[...] = x

    def chunk(j, c):
        hn = hn_ref[...]
        a = jnp.dot(hn, wg_ref[j], preferred_element_type=F32)
        b = jnp.dot(hn, wu_ref[j], preferred_element_type=F32)
        h = (a * jax.nn.sigmoid(a) * b).astype(BF)
        o_ref[...] += 0.5 * jnp.dot(h, wd_ref[j], preferred_element_type=F32)
        return c

    lax.fori_loop(0, n_chunks, chunk, 0)
    if final:
        o_ref[...] = _rms(o_ref[...], fg_ref[...])


def _ffn(x, g, w, final_g=None):
    n = x.shape[0]
    wg, wu, wd = w
    nch = wg.shape[0]
    tm = _row_tile(n, 1024)
    fg = g if final_g is None else final_g
    return pl.pallas_call(
        functools.partial(_ffn_body, n_chunks=nch, final=final_g is not None),
        out_shape=jax.ShapeDtypeStruct((n, D_MODEL), F32),
        grid=(n // tm,),
        in_specs=[pl.BlockSpec((tm, D_MODEL), lambda i: (i, 0)),
                  _full((1, D_MODEL)), _full(wg.shape), _full(wu.shape), _full(wd.shape),
                  _full((1, D_MODEL))],
        out_specs=pl.BlockSpec((tm, D_MODEL), lambda i: (i, 0)),
        scratch_shapes=[pltpu.VMEM((tm, D_MODEL), BF)],
        compiler_params=_cp(1),
        name="ffn",
    )(x, g.reshape(1, D_MODEL), wg, wu, wd, fg.reshape(1, D_MODEL))


def _inproj_even_body(x_ref, g_ref, w_ref, u_ref, q_ref, k_ref, v_ref, *bf_refs):
    hn = _rms(x_ref[...], g_ref[...]).astype(BF)
    W = GROUP_WIDTH

    def col(c):
        return jnp.dot(hn, w_ref[:, c * W:(c + 1) * W], preferred_element_type=F32)

    u_ref[...] = col(0)
    q_ref[...] = (col(1) * ATTN_SCALE).astype(q_ref.dtype)
    k = col(2)
    v = col(3)
    k_ref[...] = k
    v_ref[...] = v
    if bf_refs:
        bf_refs[0][...] = k.astype(BF)
        bf_refs[1][...] = v.astype(BF)


def _inproj_even(x, g, w, prompt):
    n = x.shape[0]
    tm = _row_tile(n, 512)
    W = GROUP_WIDTH
    spec = pl.BlockSpec((tm, W), lambda i: (i, 0))
    sd = lambda dt: jax.ShapeDtypeStruct((n, W), dt)
    outs = [sd(F32), sd(BF if prompt else F32), sd(F32), sd(F32)]
    if prompt:
        outs += [sd(BF), sd(BF)]
    return pl.pallas_call(
        _inproj_even_body,
        out_shape=outs,
        grid=(n // tm,),
        in_specs=[pl.BlockSpec((tm, D_MODEL), lambda i: (i, 0)), _full((1, D_MODEL)), _full(w.shape)],
        out_specs=[spec] * len(outs),
        compiler_params=_cp(1),
        name="inproj_even",
    )(x, g.reshape(1, D_MODEL), w)


def _split3(x):
    h = x.astype(BF)
    r = x - h.astype(F32)
    m = r.astype(BF)
    lo = (r - m.astype(F32)).astype(BF)
    return h, m, lo


def _dot3(pieces, w, dims=None):
    h, m, lo = pieces
    if dims is None:
        d = lambda a: jnp.dot(a, w, preferred_element_type=F32)
    else:
        d = lambda a: lax.dot_general(a, w, dims, preferred_element_type=F32)
    return (d(lo) + d(m)) + d(h)


def _log_sigmoid(z):
    return -(jnp.maximum(-z, 0.0) + jnp.log1p(jnp.exp(-jnp.abs(z))))


CUM_BLOCK = 256


def _inproj_odd_body(x_ref, g_ref, w_ref, wf_ref, bf_ref, xb_ref, gate_ref, q_ref, k_ref, v_ref,
                     lf_ref, *rest, tiles_per_seq):
    hn = _rms(x_ref[...], g_ref[...]).astype(BF)
    W = GROUP_WIDTH

    def col(c):
        return jnp.dot(hn, w_ref[:, c * W:(c + 1) * W], preferred_element_type=F32)

    xb_ref[...] = col(0)
    gate_ref[...] = col(1)
    q_ref[...] = (col(2) * ATTN_SCALE).astype(q_ref.dtype)
    k = col(3)
    v = col(4)
    k_ref[...] = k
    v_ref[...] = v
    f = jnp.dot(hn, wf_ref[...], preferred_element_type=F32) + bf_ref[...]
    lf = _log_sigmoid(f)
    lf_ref[...] = lf[:, :N_FOX_HEADS]
    if rest:
        kb_ref, vb_ref, cum_ref, carry_ref = rest
        kb_ref[...] = k.astype(BF)
        vb_ref[...] = v.astype(BF)

        @pl.when(pl.program_id(0) % tiles_per_seq == 0)
        def _():
            carry_ref[...] = jnp.zeros_like(carry_ref)

        tm = lf.shape[0]
        nb = tm // CUM_BLOCK
        r = lax.broadcasted_iota(jnp.int32, (CUM_BLOCK, CUM_BLOCK), 0)
        c = lax.broadcasted_iota(jnp.int32, (CUM_BLOCK, CUM_BLOCK), 1)
        tril = jnp.where(c <= r, 1.0, 0.0).astype(BF)
        carry = carry_ref[0:1, :]
        for b in range(nb):
            blk = lf[b * CUM_BLOCK:(b + 1) * CUM_BLOCK, :]
            h, m, lo = _split3(blk)
            d = lambda a: jnp.dot(tril, a, preferred_element_type=F32)
            cum = ((d(lo) + d(m)) + d(h)) + carry
            cum_ref[b * CUM_BLOCK:(b + 1) * CUM_BLOCK, :] = cum[:, :N_FOX_HEADS]
            carry = cum[CUM_BLOCK - 1:CUM_BLOCK, :]
        carry_ref[0:1, :] = carry


def _inproj_odd(x, g, w, wf, bf, prompt, seq_len):
    n = x.shape[0]
    tm = _row_tile(n, 512)
    W = GROUP_WIDTH
    H = N_FOX_HEADS
    spec = pl.BlockSpec((tm, W), lambda i: (i, 0))
    spec8 = pl.BlockSpec((tm, H), lambda i: (i, 0))
    sd = lambda dt: jax.ShapeDtypeStruct((n, W), dt)
    outs = [sd(F32), sd(F32), sd(BF if prompt else F32), sd(F32), sd(F32),
            jax.ShapeDtypeStruct((n, H), F32)]
    specs = [spec] * 5 + [spec8]
    scratch = []
    tiles_per_seq = 1
    if prompt:
        assert seq_len % tm == 0 and tm % CUM_BLOCK == 0
        tiles_per_seq = seq_len // tm
        outs += [sd(BF), sd(BF), jax.ShapeDtypeStruct((n, H), F32)]
        specs += [spec, spec, spec8]
        scratch = [pltpu.VMEM((8, 128), F32)]
    return pl.pallas_call(
        functools.partial(_inproj_odd_body, tiles_per_seq=tiles_per_seq),
        out_shape=outs,
        grid=(n // tm,),
        in_specs=[pl.BlockSpec((tm, D_MODEL), lambda i: (i, 0)), _full((1, D_MODEL)),
                  _full(w.shape), _full(wf.shape), _full((1, 128))],
        out_specs=specs,
        scratch_shapes=scratch,
        compiler_params=_cp(1),
        name="inproj_odd",
    )(x, g.reshape(1, D_MODEL), w, wf, bf)


def _outproj_body(x_ref, a_ref, b_ref, wa_ref, wb_ref, o_ref):
    o_ref[...] = (x_ref[...]
                  + jnp.dot(a_ref[...].astype(BF), wa_ref[...], preferred_element_type=F32)
                  + jnp.dot(b_ref[...].astype(BF), wb_ref[...], preferred_element_type=F32))


def _outproj(x, a, b, wa, wb):
    n = x.shape[0]
    tm = _row_tile(n, 1024)
    W = GROUP_WIDTH
    return pl.pallas_call(
        _outproj_body,
        out_shape=jax.ShapeDtypeStruct((n, D_MODEL), F32),
        grid=(n // tm,),
        in_specs=[pl.BlockSpec((tm, D_MODEL), lambda i: (i, 0)),
                  pl.BlockSpec((tm, W), lambda i: (i, 0)), pl.BlockSpec((tm, W), lambda i: (i, 0)),
                  _full(wa.shape), _full(wb.shape)],
        out_specs=pl.BlockSpec((tm, D_MODEL), lambda i: (i, 0)),
        compiler_params=_cp(1),
        name="outproj",
    )(x, a, b, wa, wb)


def _softmax_update(s, v, m, l, a):
    mn = jnp.maximum(m, jnp.max(s, axis=-1, keepdims=True))
    al = jnp.exp(m - mn)
    p = jnp.exp(s - mn)
    l = al * l + jnp.sum(p, axis=-1, keepdims=True)
    a = al * a + jnp.dot(p.astype(BF), v, preferred_element_type=F32)
    return mn, l, a


def _attn_body(lam_ref, q_ref, kT_ref, v_ref, cq_ref, ck_ref, g_ref, o_ref, *, fox, tq, out_scale):
    qi = pl.program_id(2)
    q = q_ref[0]
    lane = lax.broadcasted_iota(jnp.int32, (tq, 128), 1)
    lo = lane < DH
    zero = jnp.zeros_like(q)
    qa = jnp.where(lo, q, zero)
    qb = jnp.where(lo, zero, q)
    if fox:
        cq = cq_ref[0, 0]
        ca = cq[:, 0:1]
        cb = cq[:, 1:2]
    row = lax.broadcasted_iota(jnp.int32, (tq, tq), 0)
    colm = lax.broadcasted_iota(jnp.int32, (tq, tq), 1)
    causal = colm <= row

    def chunk(c, carry, diag):
        m1, l1, a1, m2, l2, a2 = carry
        start = pl.multiple_of(c * tq, tq)
        kT = kT_ref[0, :, pl.ds(start, tq)]
        v = v_ref[0, pl.ds(start, tq), :]
        s1 = jnp.dot(qa, kT, preferred_element_type=F32)
        s2 = jnp.dot(qb, kT, preferred_element_type=F32)
        if fox:
            ck = ck_ref[0, 0, :, pl.ds(start, tq)]
            s1 = s1 + (ca - ck[0:1, :])
            s2 = s2 + (cb - ck[1:2, :])
        if diag:
            s1 = jnp.where(causal, s1, NEG)
            s2 = jnp.where(causal, s2, NEG)
        m1, l1, a1 = _softmax_update(s1, v, m1, l1, a1)
        m2, l2, a2 = _softmax_update(s2, v, m2, l2, a2)
        return m1, l1, a1, m2, l2, a2

    neg = jnp.full((tq, 1), NEG, F32)
    z1 = jnp.zeros((tq, 1), F32)
    za = jnp.zeros((tq, 128), F32)
    carry = (neg, z1, za, neg, z1, za)
    carry = lax.fori_loop(0, qi, lambda c, cr: chunk(c, cr, False), carry)
    m1, l1, a1, m2, l2, a2 = chunk(qi, carry, True)
    o1 = a1 / l1
    o2 = a2 / l2
    if fox:
        o_ref[0] = jnp.where(lo, o1, o2)
    else:
        o = o1 - lam_ref[0] * o2
        o_ref[0] = _rms(o, g_ref[...]) * out_scale


def _attn_prompt(q, kT, v, lam, cq, ck, g, *, fox, out_scale):
    B, T, _ = q.shape
    tq = 256 if T % 256 == 0 else T
    npair = GROUP_WIDTH // 128
    return pl.pallas_call(
        functools.partial(_attn_body, fox=fox, tq=tq, out_scale=out_scale),
        out_shape=jax.ShapeDtypeStruct((B, T, GROUP_WIDTH), F32),
        grid=(B, npair, T // tq),
        in_specs=[pl.BlockSpec(memory_space=pltpu.SMEM),
                  pl.BlockSpec((1, tq, 128), lambda b, j, i: (b, i, j)),
                  pl.BlockSpec((1, 128, T), lambda b, j, i: (b, j, 0)),
                  pl.BlockSpec((1, T, 128), lambda b, j, i: (b, 0, j)),
                  pl.BlockSpec((1, 1, tq, 2), lambda b, j, i: (b, j, i, 0)) if fox
                  else pl.BlockSpec((1, 1, 8, 2), lambda b, j, i: (0, 0, 0, 0)),
                  pl.BlockSpec((1, 1, 2, T), lambda b, j, i: (b, j, 0, 0)) if fox
                  else pl.BlockSpec((1, 1, 2, 128), lambda b, j, i: (0, 0, 0, 0)),
                  pl.BlockSpec((1, 128), lambda b, j, i: (0, 0))],
        out_specs=pl.BlockSpec((1, tq, 128), lambda b, j, i: (b, i, j)),
        compiler_params=_cp(3),
        name="attn_fox" if fox else "attn_diff",
    )(lam, q, kT, v, cq, ck, g)


def _page_softmax_update(sc, vf, m, l, acc):
    mn = jnp.maximum(m, jnp.max(sc, axis=-1, keepdims=True))
    al = jnp.exp(m - mn)
    p = jnp.exp(sc - mn)
    l = al * l + jnp.sum(p, axis=-1, keepdims=True)
    acc = al * acc + jnp.dot(p.astype(BF), vf.astype(BF), preferred_element_type=F32)
    return mn, l, acc


_NT = (((1,), (1,)), ((), ()))


def _fox_decode_body(pt_ref, q_ref, kn_ref, vn_ref, lfn_ref, wr_ref, *rest, G):
    k_refs, v_refs, lf_refs = rest[0:G], rest[G:2 * G], rest[2 * G:3 * G]
    o_ref = rest[3 * G]
    m_scr, l_scr, acc_scr, car_scr = rest[3 * G + 1:]
    s = pl.program_id(1)
    q = q_ref[0]

    @pl.when(s == 0)
    def _():
        m_scr[...] = jnp.sum(q * kn_ref[0], axis=-1, keepdims=True)
        l_scr[...] = jnp.ones_like(l_scr)
        acc_scr[...] = vn_ref[0]
        car_scr[...] = lfn_ref[0]

    qb = q.astype(BF)
    rows = k_refs[0].shape[0]
    sub = lax.broadcasted_iota(jnp.int32, (N_FOX_HEADS, rows), 0)
    ln = lax.broadcasted_iota(jnp.int32, (N_FOX_HEADS, rows), 1)
    valid = (ln & (N_FOX_HEADS - 1)) == sub
    m, l, acc, car = m_scr[...], l_scr[...], acc_scr[...], car_scr[...]
    wr = wr_ref[...]
    for g in range(G):
        kf = k_refs[g][...].astype(BF)
        sc = lax.dot_general(qb, kf, _NT, preferred_element_type=F32)
        lfT = lf_refs[g][...]
        bias = _dot3(_split3(lfT), wr)
        sc = jnp.where(valid, sc + (bias + car), NEG)
        m, l, acc = _page_softmax_update(sc, v_refs[g][...], m, l, acc)
        car = car + jnp.sum(lfT, axis=-1, keepdims=True)
    m_scr[...], l_scr[...], acc_scr[...], car_scr[...] = m, l, acc, car

    @pl.when(s == pl.num_programs(1) - 1)
    def _():
        o_ref[0] = acc / l


def _diff_decode_body(pt_ref, lam_ref, q_ref, kn_ref, vn_ref, g_ref, *rest, G, out_scale):
    k_refs, v_refs = rest[0:G], rest[G:2 * G]
    o_ref = rest[2 * G]
    m_scr, l_scr, acc_scr = rest[2 * G + 1:]
    s = pl.program_id(1)
    q = q_ref[0]
    nh = 4

    @pl.when(s == 0)
    def _():
        m_scr[...] = jnp.sum(q * kn_ref[0], axis=-1, keepdims=True)
        l_scr[...] = jnp.ones_like(l_scr)
        acc_scr[...] = vn_ref[0]

    qb = q.astype(BF)
    half = k_refs[0].shape[0] // 2
    sub = lax.broadcasted_iota(jnp.int32, (2 * nh, half), 0)
    ln = lax.broadcasted_iota(jnp.int32, (2 * nh, half), 1)
    valid = (ln & (nh - 1)) == (sub & (nh - 1))
    first = sub < nh
    m, l, acc = m_scr[...], l_scr[...], acc_scr[...]
    for g in range(G):
        k1 = k_refs[g][pl.ds(0, half, stride=2), :].astype(BF)
        k2 = k_refs[g][pl.ds(1, half, stride=2), :].astype(BF)
        s1 = lax.dot_general(qb, k1, _NT, preferred_element_type=F32)
        s2 = lax.dot_general(qb, k2, _NT, preferred_element_type=F32)
        sc = jnp.where(valid, jnp.where(first, s1, s2), NEG)
        m, l, acc = _page_softmax_update(sc, v_refs[g][...], m, l, acc)
    m_scr[...], l_scr[...], acc_scr[...] = m, l, acc

    @pl.when(s == pl.num_programs(1) - 1)
    def _():
        o = acc / l
        o = o[0:nh, :] - lam_ref[0] * o[nh:2 * nh, :]
        o_ref[0] = _rms(o, g_ref[...]) * out_scale


def _decode_scratch(width):
    return [pltpu.VMEM((8, 1), F32), pltpu.VMEM((8, 1), F32), pltpu.VMEM((8, width), F32)]


def _fox_decode(page_table, layer, q, kn, vn, lfn, cache_k, cache_v, cache_lfT):
    B, n_pages = page_table.shape
    G = math.gcd(PAGES_PER_STEP, n_pages)
    rows = cache_k.shape[2]
    page_rows = cache_lfT.shape[3]
    r_src = lax.broadcasted_iota(jnp.int32, (page_rows, rows), 0)
    r_dst = lax.broadcasted_iota(jnp.int32, (page_rows, rows), 1) // N_FOX_HEADS
    wr = jnp.where(r_src > r_dst, 1.0, 0.0).astype(BF)

    def page(g):
        return lambda b, s, pt: (layer, pt[b, n_pages - 1 - (s * G + g)], 0, 0)

    tok = lambda w: pl.BlockSpec((1, 8, w), lambda b, s, pt: (b, 0, 0))
    in_specs = ([tok(DH), tok(DH), tok(DH), tok(1), pl.BlockSpec(wr.shape, lambda b, s, pt: (0, 0))]
                + [pl.BlockSpec((None, None, rows, DH), page(g)) for g in range(G)]
                + [pl.BlockSpec((None, None, rows, DH), page(g)) for g in range(G)]
                + [pl.BlockSpec((None, None, N_FOX_HEADS, page_rows), page(g)) for g in range(G)])
    return pl.pallas_call(
        functools.partial(_fox_decode_body, G=G),
        out_shape=jax.ShapeDtypeStruct((B, 8, DH), F32),
        grid_spec=pltpu.PrefetchScalarGridSpec(
            num_scalar_prefetch=1, grid=(B, n_pages // G), in_specs=in_specs,
            out_specs=tok(DH), scratch_shapes=_decode_scratch(DH) + [pltpu.VMEM((8, 1), F32)]),
        compiler_params=_cp(2),
        name="fox_decode",
    )(page_table, q, kn, vn, lfn, wr, *([cache_k] * G), *([cache_v] * G), *([cache_lfT] * G))


def _diff_decode(page_table, layer, lam, q, kn, vn, g, cache_k, cache_v, out_scale):
    B, n_pages = page_table.shape
    G = math.gcd(PAGES_PER_STEP, n_pages)
    krows, vrows = cache_k.shape[2], cache_v.shape[2]

    def page(g_):
        return lambda b, s, pt: (layer, pt[b, s * G + g_], 0, 0)

    tok = lambda r, w: pl.BlockSpec((1, r, w), lambda b, s, pt: (b, 0, 0))
    in_specs = ([pl.BlockSpec(memory_space=pltpu.SMEM), tok(8, DH), tok(8, DH), tok(8, 128),
                 pl.BlockSpec((1, 128), lambda b, s, pt: (0, 0))]
                + [pl.BlockSpec((None, None, krows, DH), page(g_)) for g_ in range(G)]
                + [pl.BlockSpec((None, None, vrows, 128), page(g_)) for g_ in range(G)])
    return pl.pallas_call(
        functools.partial(_diff_decode_body, G=G, out_scale=out_scale),
        out_shape=jax.ShapeDtypeStruct((B, 4, 128), F32),
        grid_spec=pltpu.PrefetchScalarGridSpec(
            num_scalar_prefetch=1, grid=(B, n_pages // G), in_specs=in_specs,
            out_specs=tok(4, 128), scratch_shapes=_decode_scratch(128)),
        compiler_params=_cp(2),
        name="diff_decode",
    )(page_table, lam, q, kn, vn, g, *([cache_k] * G), *([cache_v] * G))


def _s5_body(u_ref, h0r_ref, h0i_ref, ar_ref, ai_ref, bm_ref, cr_ref, ci_ref, d_ref, wg_ref,
             o_ref, hr_out, hi_out, br_scr, bi_scr, hr_scr, hi_scr, *, tc, nb):
    i = pl.program_id(0)
    LB = S5_LANES // S5_BLOCKS
    UB = GROUP_WIDTH // S5_BLOCKS

    @pl.when(i == 0)
    def _():
        hr_scr[...] = h0r_ref[...]
        hi_scr[...] = h0i_ref[...]

    u = u_ref[...]
    ys = []
    for b in range(S5_BLOCKS):
        sl = slice(b * LB, (b + 1) * LB)
        bb = jnp.dot(u[:, b * UB:(b + 1) * UB].astype(BF), bm_ref[b], preferred_element_type=F32)
        br_scr[:, sl] = bb[:, :LB]
        bi_scr[:, sl] = bb[:, LB:]
        ar = jnp.broadcast_to(ar_ref[:, sl], (nb, LB))
        ai = jnp.broadcast_to(ai_ref[:, sl], (nb, LB))

        def step(t, h, sl=sl, ar=ar, ai=ai):
            hr, hi = h
            r0 = pl.multiple_of(t * nb, nb)
            nr = ar * hr - ai * hi + br_scr[pl.ds(r0, nb), sl]
            ni = ar * hi + ai * hr + bi_scr[pl.ds(r0, nb), sl]
            br_scr[pl.ds(r0, nb), sl] = nr
            bi_scr[pl.ds(r0, nb), sl] = ni
            return nr, ni

        hr, hi = lax.fori_loop(0, tc, step, (hr_scr[:, sl], hi_scr[:, sl]))
        hr_scr[:, sl] = hr
        hi_scr[:, sl] = hi
        ys.append(jnp.dot(br_scr[:, sl].astype(BF), cr_ref[b], preferred_element_type=F32)
                  + jnp.dot(bi_scr[:, sl].astype(BF), ci_ref[b], preferred_element_type=F32))
    y = jnp.concatenate(ys, axis=-1) + d_ref[...] * u
    g = jax.nn.gelu(y)
    o_ref[...] = g * jax.nn.sigmoid(jnp.dot(g.astype(BF), wg_ref[...], preferred_element_type=F32))

    @pl.when(i == pl.num_programs(0) - 1)
    def _():
        hr_out[...] = hr_scr[...]
        hi_out[...] = hi_scr[...]


def _s5(u_t, h0r, h0i, p, T, nb):
    tc = min(T, 64)
    assert T % tc == 0
    rows = tc * nb
    st = jax.ShapeDtypeStruct((nb, S5_LANES), F32)
    return pl.pallas_call(
        functools.partial(_s5_body, tc=tc, nb=nb),
        out_shape=[jax.ShapeDtypeStruct((T * nb, GROUP_WIDTH), F32), st, st],
        grid=(T // tc,),
        in_specs=[pl.BlockSpec((rows, GROUP_WIDTH), lambda i: (i, 0)),
                  _full((nb, S5_LANES)), _full((nb, S5_LANES)),
                  _full((1, S5_LANES)), _full((1, S5_LANES)),
                  _full(p["bm"].shape), _full(p["cr"].shape), _full(p["ci"].shape),
                  _full((1, GROUP_WIDTH)), _full(p["wglu"].shape)],
        out_specs=[pl.BlockSpec((rows, GROUP_WIDTH), lambda i: (i, 0)),
                   pl.BlockSpec((nb, S5_LANES), lambda i: (0, 0)),
                   pl.BlockSpec((nb, S5_LANES), lambda i: (0, 0))],
        scratch_shapes=[pltpu.VMEM((rows, S5_LANES), F32), pltpu.VMEM((rows, S5_LANES), F32),
                        pltpu.VMEM((nb, S5_LANES), F32), pltpu.VMEM((nb, S5_LANES), F32)],
        compiler_params=_cp(1),
        name="s5",
    )(u_t, h0r, h0i, p["ar"], p["ai"], p["bm"], p["cr"], p["ci"], p["d"], p["wglu"])


def _lru_body(xb_ref, gate_ref, cv0_ref, h0_ref, cw_ref, cb_ref, wa_ref, ba_ref, wx_ref, bx_ref, nsp_ref,
              o_ref, cv_out, h_out, xp_scr, a_scr, b_scr, h_scr, *, tc, nb):
    i = pl.program_id(0)
    rows = tc * nb
    tail = (CONV_W - 1) * nb

    @pl.when(i == 0)
    def _():
        xp_scr[0:tail, :] = cv0_ref[...]
        h_scr[...] = h0_ref[...]

    @pl.when(i > 0)
    def _():
        xp_scr[0:tail, :] = xp_scr[rows:rows + tail, :]

    xp_scr[tail:tail + rows, :] = xb_ref[...]
    xc = cb_ref[...] + cw_ref[0:1, :] * xp_scr[0:rows, :]
    for j in range(1, CONV_W):
        xc = xc + cw_ref[j:j + 1, :] * xp_scr[j * nb:j * nb + rows, :]
    xcb = xc.astype(BF)
    r = jax.nn.sigmoid(jnp.dot(xcb, wa_ref[...], preferred_element_type=F32) + ba_ref[...])
    ig = jax.nn.sigmoid(jnp.dot(xcb, wx_ref[...], preferred_element_type=F32) + bx_ref[...])
    log_a = nsp_ref[...] * r
    a_scr[...] = jnp.exp(log_a)
    th = jnp.tanh(log_a)
    b_scr[...] = jnp.sqrt(-2.0 * th / (1.0 - th)) * (ig * xc)

    def step(t, h):
        r0 = pl.multiple_of(t * nb, nb)
        h = a_scr[pl.ds(r0, nb), :] * h + b_scr[pl.ds(r0, nb), :]
        b_scr[pl.ds(r0, nb), :] = h
        return h

    h = lax.fori_loop(0, tc, step, h_scr[...])
    h_scr[...] = h
    o_ref[...] = jax.nn.gelu(gate_ref[...]) * b_scr[...]

    @pl.when(i == pl.num_programs(0) - 1)
    def _():
        cv_out[...] = xp_scr[rows:rows + tail, :]
        h_out[...] = h


def _lru(xb_t, gate_t, cv0, h0, p, T, nb):
    tc = min(T, 128)
    assert T % tc == 0
    rows = tc * nb
    W = GROUP_WIDTH
    tail = (CONV_W - 1) * nb
    row_spec = pl.BlockSpec((rows, W), lambda i: (i, 0))
    vec = _full((1, W))
    return pl.pallas_call(
        functools.partial(_lru_body, tc=tc, nb=nb),
        out_shape=[jax.ShapeDtypeStruct((T * nb, W), F32), jax.ShapeDtypeStruct((tail, W), F32),
                   jax.ShapeDtypeStruct((nb, W), F32)],
        grid=(T // tc,),
        in_specs=[row_spec, row_spec, _full((tail, W)), _full((nb, W)), _full((CONV_W, W)), vec,
                  _full((W, W)), vec, _full((W, W)), vec, vec],
        out_specs=[row_spec, pl.BlockSpec((tail, W), lambda i: (0, 0)), pl.BlockSpec((nb, W), lambda i: (0, 0))],
        scratch_shapes=[pltpu.VMEM((rows + tail, W), F32), pltpu.VMEM((rows, W), F32),
                        pltpu.VMEM((rows, W), F32), pltpu.VMEM((nb, W), F32)],
        compiler_params=_cp(1),
        name="rglru",
    )(xb_t, gate_t, cv0, h0, p["cw"], p["cb"], p["wa"], p["ba"], p["wx"], p["bx"], p["nsp"])


def _prep_ffn(wg, wu, wd):
    d, f = wg.shape
    nch = f // FF_CHUNK
    cols = lambda w: w.astype(BF).reshape(d, nch, FF_CHUNK).transpose(1, 0, 2)
    return cols(wg), cols(wu), wd.astype(BF).reshape(nch, FF_CHUNK, d)


def _prep_s5(a_re, a_im, b_re, b_im, c_re, c_im, d_skip, log_dt, w_glu):
    dt = jnp.exp(log_dt)[:, None]
    mag = jnp.exp(a_re * dt)
    ab_re, ab_im = mag * jnp.cos(a_im * dt), mag * jnp.sin(a_im * dt)
    den = a_re * a_re + a_im * a_im
    nr, ni = ab_re - 1.0, ab_im
    co_re, co_im = (nr * a_re + ni * a_im) / den, (ni * a_re - nr * a_im) / den
    bt_re = co_re[..., None] * b_re - co_im[..., None] * b_im
    bt_im = co_re[..., None] * b_im + co_im[..., None] * b_re
    gpb = S5_GROUPS // S5_BLOCKS
    eye = jnp.eye(gpb, dtype=F32)

    def blk_in(bt):
        x = bt.reshape(S5_BLOCKS, gpb, S5_STATE, S5_CH)
        return jnp.einsum('bgph,gk->bghkp', x, eye).reshape(S5_BLOCKS, gpb * S5_CH, gpb * S5_STATE)

    def blk_out(c):
        x = c.reshape(S5_BLOCKS, gpb, S5_CH, S5_STATE)
        return jnp.einsum('bghp,gk->bgpkh', x, eye).reshape(S5_BLOCKS, gpb * S5_STATE, gpb * S5_CH)

    return {"ar": ab_re.reshape(1, S5_LANES), "ai": ab_im.reshape(1, S5_LANES),
            "bm": jnp.concatenate([blk_in(bt_re), blk_in(bt_im)], axis=-1).astype(BF),
            "cr": blk_out(c_re).astype(BF), "ci": blk_out(-c_im).astype(BF),
            "d": d_skip.reshape(1, GROUP_WIDTH), "wglu": w_glu.astype(BF)}


def _prep_lru(conv_w, conv_b, w_a, b_a, w_x, b_x, lam):
    eye = jnp.eye(LRU_BLOCKS, dtype=F32)
    W = GROUP_WIDTH
    dense = lambda w: jnp.einsum('nde,nm->ndme', w, eye).reshape(W, W).astype(BF)
    row = lambda v: v.reshape(1, W)
    return {"cw": conv_w, "cb": row(conv_b), "wa": dense(w_a), "ba": row(b_a), "wx": dense(w_x),
            "bx": row(b_x), "nsp": row(-LRU_C * jax.nn.softplus(-lam))}


def _to_time_major(a, B, T):
    return a.reshape(B, T, -1).transpose(1, 0, 2).reshape(T * B, -1)


def _to_batch_major(a, B, T):
    return a.reshape(T, B, -1).transpose(1, 0, 2).reshape(B * T, -1)


def _trunk(x, B, T, P, st):
    prompt = st is None
    W = GROUP_WIDTH
    kd, vd, sr, si, kf, vf, lf, cv, lr = ([] for _ in range(9))
    depth = len(P["ffn"])
    for l in range(depth):
        g = P["norm_g"][l]
        x = _ffn(x, g[0], P["ffn"][l][0])
        if l % 2 == 0:
            e = l // 2
            lam_init = 0.8 - 0.6 * math.exp(-0.3 * l)
            lam = P["lam"][e]
            res = _inproj_even(x, g[1], P["w_in_even"][e], prompt)
            u, q, k, v = res[:4]
            if prompt:
                kb, vb = res[4:]
                kT = kb.reshape(B, T, W).transpose(0, 2, 1)
                attn = _attn_prompt(q.reshape(B, T, W), kT, vb.reshape(B, T, W), lam,
                                    P["dummy_cq"], P["dummy_ck"], P["subln_g"][e],
                                    fox=False, out_scale=1.0 - lam_init).reshape(B * T, W)
                h0r = h0i = jnp.zeros((B, S5_LANES), F32)
                u_t = _to_time_major(u, B, T)
            else:
                nh = W // (2 * DH)
                mh = lambda a: a.reshape(B, nh, 2, DH).transpose(0, 2, 1, 3).reshape(B, 2 * nh, DH)
                vn = v.reshape(B, nh, 2 * DH)
                attn = _diff_decode(st["page_table"], e, lam, mh(q), mh(k),
                                    jnp.concatenate([vn, vn], axis=1), P["subln_g"][e],
                                    st["k_diff"], st["v_diff"], 1.0 - lam_init).reshape(B, W)
                h0r = st["s5_re"][e].reshape(B, S5_LANES)
                h0i = st["s5_im"][e].reshape(B, S5_LANES)
                u_t = u
            s5_t, hr, hi = _s5(u_t, h0r, h0i, P["s5"][e], T, B)
            rec = _to_batch_major(s5_t, B, T) if prompt else s5_t
            x = _outproj(x, rec, attn, *P["w_out_even"][e])
            kd.append(k.reshape(B, T, 2 * (W // (2 * DH)), DH))
            vd.append(v.reshape(B, T, W // (2 * DH), 2 * DH))
            sr.append(hr.reshape(B, S5_GROUPS, S5_STATE))
            si.append(hi.reshape(B, S5_GROUPS, S5_STATE))
        else:
            o = l // 2
            H = N_FOX_HEADS
            res = _inproj_odd(x, g[1], P["w_in_odd"][o], P["w_f"][o], P["b_f"][o], prompt, T)
            xb, gate, q, k, v, logf = res[:6]
            if prompt:
                kb, vb, cum = res[6:]
                kT = kb.reshape(B, T, W).transpose(0, 2, 1)
                cum4 = cum.reshape(B, T, H // 2, 2)
                attn = _attn_prompt(q.reshape(B, T, W), kT, vb.reshape(B, T, W), P["lam"][0],
                                    cum4.transpose(0, 2, 1, 3), cum4.transpose(0, 2, 3, 1), P["subln_g"][0],
                                    fox=True, out_scale=1.0).reshape(B * T, W)
                cv0 = jnp.zeros(((CONV_W - 1) * B, W), F32)
                h0 = jnp.zeros((B, W), F32)
                xb_t, gate_t = _to_time_major(xb, B, T), _to_time_major(gate, B, T)
            else:
                hd = lambda a: a.reshape(B, H, DH)
                attn = _fox_decode(st["page_table"], o, hd(q), hd(k), hd(v), logf.reshape(B, H, 1),
                                   st["k_fox"], st["v_fox"], st["logfT"]).reshape(B, W)
                cv0 = st["conv"][o].transpose(1, 0, 2).reshape((CONV_W - 1) * B, W)
                h0 = st["lru"][o]
                xb_t, gate_t = xb, gate
            y_t, cv_new, h_new = _lru(xb_t, gate_t, cv0, h0, P["lru"][o], T, B)
            rec = _to_batch_major(y_t, B, T) if prompt else y_t
            x = _outproj(x, rec, attn, *P["w_out_odd"][o])
            kf.append(k.reshape(B, T, H, DH))
            vf.append(v.reshape(B, T, H, DH))
            lf.append(logf.reshape(B, T, H))
            cv.append(cv_new.reshape(CONV_W - 1, B, W).transpose(1, 0, 2))
            lr.append(h_new)
        x = _ffn(x, g[2], P["ffn"][l][1], final_g=P["final_norm_g"] if l == depth - 1 else None)
    return (x.reshape(B, T, D_MODEL), jnp.stack(kd), jnp.stack(vd), jnp.stack(sr), jnp.stack(si),
            jnp.stack(kf), jnp.stack(vf), jnp.stack(lf), jnp.stack(cv), jnp.stack(lr))


def kernel(x_prompt, x_sample, cache_k_diff, cache_v_diff, state_s5_re, state_s5_im, cache_k_fox, cache_v_fox, cache_logf_fox, state_conv, state_lru, page_table, norm_g, final_norm_g, w_ffn_gate, w_ffn_up, w_ffn_down, w_in_even, w_out_even, s5_a_re, s5_a_im, s5_b_re, s5_b_im, s5_c_re, s5_c_im, s5_d, s5_log_dt, s5_w_glu, lambda_q1, lambda_k1, lambda_q2, lambda_k2, diff_subln_g, w_in_odd, w_out_odd, fox_b_f, conv_w, conv_b, lru_w_a, lru_b_a, lru_w_x, lru_b_x, lru_lambda):
    depth = w_ffn_gate.shape[0]
    n_even, n_odd = w_in_even.shape[0], w_in_odd.shape[0]
    W = GROUP_WIDTH
    H = N_FOX_HEADS
    main = 2 * W + 3 * W
    lam = (jnp.exp(jnp.sum(lambda_q1 * lambda_k1, axis=-1)) - jnp.exp(jnp.sum(lambda_q2 * lambda_k2, axis=-1)))
    lam_init = jnp.array([0.8 - 0.6 * math.exp(-0.3 * 2 * e) for e in range(n_even)], F32)
    P = {
        "norm_g": norm_g, "final_norm_g": final_norm_g,
        "ffn": [[_prep_ffn(w_ffn_gate[l, i], w_ffn_up[l, i], w_ffn_down[l, i]) for i in range(2)]
                for l in range(depth)],
        "w_in_even": w_in_even.astype(BF),
        "w_out_even": [(w_out_even[e, :W].astype(BF), w_out_even[e, W:].astype(BF)) for e in range(n_even)],
        "s5": [_prep_s5(s5_a_re[e], s5_a_im[e], s5_b_re[e], s5_b_im[e], s5_c_re[e], s5_c_im[e],
                        s5_d[e], s5_log_dt[e], s5_w_glu[e]) for e in range(n_even)],
        "lam": (lam + lam_init).reshape(n_even, 1),
        "subln_g": diff_subln_g.reshape(n_even, 1, 2 * DH),
        "w_in_odd": w_in_odd[:, :, :main].astype(BF),
        "w_f": jnp.pad(w_in_odd[:, :, main:], ((0, 0), (0, 0), (0, 128 - H))).astype(BF),
        "b_f": jnp.pad(fox_b_f, ((0, 0), (0, 128 - H))).reshape(n_odd, 1, 128),
        "w_out_odd": [(w_out_odd[o, :W].astype(BF), w_out_odd[o, W:].astype(BF)) for o in range(n_odd)],
        "lru": [_prep_lru(conv_w[o], conv_b[o], lru_w_a[o], lru_b_a[o], lru_w_x[o], lru_b_x[o], lru_lambda[o])
                for o in range(n_odd)],
        "dummy_cq": jnp.zeros((1, 1, 8, 2), F32), "dummy_ck": jnp.zeros((1, 1, 2, 128), F32),
    }
    Bp, Tp, _ = x_prompt.shape
    Bs, Ts, _ = x_sample.shape
    n_phys, page = cache_k_diff.shape[1], cache_k_diff.shape[2]
    st = {
        "page_table": page_table,
        "k_diff": cache_k_diff.reshape(n_even, n_phys, page * 2 * (W // (2 * DH)), DH),
        "v_diff": cache_v_diff.reshape(n_even, n_phys, page * (W // (2 * DH)), 2 * DH),
        "k_fox": cache_k_fox.reshape(n_odd, n_phys, page * H, DH),
        "v_fox": cache_v_fox.reshape(n_odd, n_phys, page * H, DH),
        "logfT": cache_logf_fox.transpose(0, 1, 3, 2),
        "s5_re": state_s5_re, "s5_im": state_s5_im, "conv": state_conv, "lru": state_lru,
    }
    outs_p = _trunk(x_prompt.reshape(Bp * Tp, D_MODEL), Bp, Tp, P, None)
    outs_s = _trunk(x_sample.reshape(Bs * Ts, D_MODEL), Bs, Ts, P, st)
    return (outs_p[0], outs_s[0]) + tuple(outs_p[1:]) + tuple(outs_s[1:])
```

```python
import functools
import math

import jax
import jax.numpy as jnp
from jax import lax
from jax.experimental import pallas as pl
from jax.experimental.pallas import tpu as pltpu

F32 = jnp.float32
BF = jnp.bfloat16

D_MODEL = 1024
GROUP_WIDTH = 512
DH = 64
S5_GROUPS = 32
S5_STATE = 64
S5_CH = 16
S5_LANES = S5_GROUPS * S5_STATE
S5_BLOCKS = 4
LRU_BLOCKS = 8
LRU_C = 8.0
CONV_W = 4
N_FOX_HEADS = 8
FF_CHUNK = 256
EPS = 1e-6
NEG = -1e30
ATTN_SCALE = DH ** -0.5
LOG2E = math.log2(math.e)
PROMPT_Q_SCALE = ATTN_SCALE * LOG2E

VMEM_LIMIT_V7X = 56 * 1024 * 1024
PAGES_PER_STEP = 8


def _cp(n_axes, vmem=VMEM_LIMIT_V7X):
    return pltpu.CompilerParams(dimension_semantics=("arbitrary",) * n_axes,
                                vmem_limit_bytes=vmem)


def _rms(x, g):
    return x * lax.rsqrt(jnp.mean(x * x, axis=-1, keepdims=True) + EPS) * g


def _full(shape, single=True):
    nd = len(shape)
    kw = {"pipeline_mode": pl.Buffered(1)} if single else {}
    return pl.BlockSpec(shape, lambda *_: (0,) * nd, **kw)


def _row_tile(n, want):
    t = min(n, want)
    assert n % t == 0
    return t


def _ffn_body(x_ref, g_ref, wg_ref, wu_ref, wd_ref, fg_ref, o_ref, hn_ref, *, n_chunks, final):
    x = x_ref[...]
    hn_ref[...] = _rms(x, g_ref[...]).astype(BF)
    o_ref[...] = x

    def chunk(j, c):
        hn = hn_ref[...]
        a = jnp.dot(hn, wg_ref[j], preferred_element_type=F32)
        b = jnp.dot(hn, wu_ref[j], preferred_element_type=F32)
        h = (a * jax.nn.sigmoid(a) * b).astype(BF)
        o_ref[...] += 0.5 * jnp.dot(h, wd_ref[j], preferred_element_type=F32)
        return c

    lax.fori_loop(0, n_chunks, chunk, 0)
    if final:
        o_ref[...] = _rms(o_ref[...], fg_ref[...])


def _ffn(x, g, w, final_g=None):
    n = x.shape[0]
    wg, wu, wd = w
    nch = wg.shape[0]
    tm = _row_tile(n, 1024)
    fg = g if final_g is None else final_g
    return pl.pallas_call(
        functools.partial(_ffn_body, n_chunks=nch, final=final_g is not None),
        out_shape=jax.ShapeDtypeStruct((n, D_MODEL), F32),
        grid=(n // tm,),
        in_specs=[pl.BlockSpec((tm, D_MODEL), lambda i: (i, 0)),
                  _full((1, D_MODEL)), _full(wg.shape), _full(wu.shape), _full(wd.shape),
                  _full((1, D_MODEL))],
        out_specs=pl.BlockSpec((tm, D_MODEL), lambda i: (i, 0)),
        scratch_shapes=[pltpu.VMEM((tm, D_MODEL), BF)],
        compiler_params=_cp(1),
        name="ffn",
    )(x, g.reshape(1, D_MODEL), wg, wu, wd, fg.reshape(1, D_MODEL))


def _inproj_even_body(x_ref, g_ref, w_ref, u_ref, q_ref, k_ref, v_ref, *bf_refs, q_scale):
    hn = _rms(x_ref[...], g_ref[...]).astype(BF)
    W = GROUP_WIDTH

    def col(c):
        return jnp.dot(hn, w_ref[:, c * W:(c + 1) * W], preferred_element_type=F32)

    u_ref[...] = col(0)
    q_ref[...] = (col(1) * q_scale).astype(q_ref.dtype)
    k = col(2)
    v = col(3)
    k_ref[...] = k
    v_ref[...] = v
    if bf_refs:
        bf_refs[0][...] = k.astype(BF)
        bf_refs[1][...] = v.astype(BF)


def _inproj_even(x, g, w, prompt):
    n = x.shape[0]
    tm = _row_tile(n, 512)
    W = GROUP_WIDTH
    spec = pl.BlockSpec((tm, W), lambda i: (i, 0))
    sd = lambda dt: jax.ShapeDtypeStruct((n, W), dt)
    outs = [sd(F32), sd(BF if prompt else F32), sd(F32), sd(F32)]
    if prompt:
        outs += [sd(BF), sd(BF)]
    return pl.pallas_call(
        functools.partial(_inproj_even_body, q_scale=PROMPT_Q_SCALE if prompt else ATTN_SCALE),
        out_shape=outs,
        grid=(n // tm,),
        in_specs=[pl.BlockSpec((tm, D_MODEL), lambda i: (i, 0)), _full((1, D_MODEL)), _full(w.shape)],
        out_specs=[spec] * len(outs),
        compiler_params=_cp(1),
        name="inproj_even",
    )(x, g.reshape(1, D_MODEL), w)


def _split3(x):
    h = x.astype(BF)
    r = x - h.astype(F32)
    m = r.astype(BF)
    lo = (r - m.astype(F32)).astype(BF)
    return h, m, lo


def _dot3(pieces, w, dims=None):
    h, m, lo = pieces
    if dims is None:
        d = lambda a: jnp.dot(a, w, preferred_element_type=F32)
    else:
        d = lambda a: lax.dot_general(a, w, dims, preferred_element_type=F32)
    return (d(lo) + d(m)) + d(h)


def _log_sigmoid(z):
    return -(jnp.maximum(-z, 0.0) + jnp.log1p(jnp.exp(-jnp.abs(z))))


CUM_BLOCK = 256


def _inproj_odd_body(x_ref, g_ref, w_ref, wf_ref, bf_ref, xb_ref, gate_ref, q_ref, k_ref, v_ref,
                     lf_ref, *rest, tiles_per_seq, q_scale):
    hn = _rms(x_ref[...], g_ref[...]).astype(BF)
    W = GROUP_WIDTH

    def col(c):
        return jnp.dot(hn, w_ref[:, c * W:(c + 1) * W], preferred_element_type=F32)

    xb_ref[...] = col(0)
    gate_ref[...] = col(1)
    q_ref[...] = (col(2) * q_scale).astype(q_ref.dtype)
    k = col(3)
    v = col(4)
    k_ref[...] = k
    v_ref[...] = v
    f = jnp.dot(hn, wf_ref[...], preferred_element_type=F32) + bf_ref[...]
    lf = _log_sigmoid(f)
    lf_ref[...] = lf[:, :N_FOX_HEADS]
    if rest:
        kb_ref, vb_ref, ch_ref, cm_ref, cl_ref, carry_ref = rest
        kb_ref[...] = k.astype(BF)
        vb_ref[...] = v.astype(BF)

        @pl.when(pl.program_id(0) % tiles_per_seq == 0)
        def _():
            carry_ref[...] = jnp.zeros_like(carry_ref)

        tm = lf.shape[0]
        nb = tm // CUM_BLOCK
        r = lax.broadcasted_iota(jnp.int32, (CUM_BLOCK, CUM_BLOCK), 0)
        c = lax.broadcasted_iota(jnp.int32, (CUM_BLOCK, CUM_BLOCK), 1)
        tril = jnp.where(c <= r, 1.0, 0.0).astype(BF)
        carry = carry_ref[0:1, :]
        for b in range(nb):
            blk = lf[b * CUM_BLOCK:(b + 1) * CUM_BLOCK, :]
            h, m, lo = _split3(blk)
            d = lambda a: jnp.dot(tril, a, preferred_element_type=F32)
            cum = ((d(lo) + d(m)) + d(h)) + carry
            rows = slice(b * CUM_BLOCK, (b + 1) * CUM_BLOCK)
            for piece, ref in zip(_split3(cum * (-LOG2E)), (ch_ref, cm_ref, cl_ref)):
                ref[rows, :] = piece[:, :N_FOX_HEADS]
            carry = cum[CUM_BLOCK - 1:CUM_BLOCK, :]
        carry_ref[0:1, :] = carry


def _inproj_odd(x, g, w, wf, bf, prompt, seq_len):
    n = x.shape[0]
    tm = _row_tile(n, 512)
    W = GROUP_WIDTH
    H = N_FOX_HEADS
    spec = pl.BlockSpec((tm, W), lambda i: (i, 0))
    spec8 = pl.BlockSpec((tm, H), lambda i: (i, 0))
    sd = lambda dt: jax.ShapeDtypeStruct((n, W), dt)
    outs = [sd(F32), sd(F32), sd(BF if prompt else F32), sd(F32), sd(F32),
            jax.ShapeDtypeStruct((n, H), F32)]
    specs = [spec] * 5 + [spec8]
    scratch = []
    tiles_per_seq = 1
    if prompt:
        assert seq_len % tm == 0 and tm % CUM_BLOCK == 0
        tiles_per_seq = seq_len // tm
        outs += [sd(BF), sd(BF)] + [jax.ShapeDtypeStruct((n, H), BF)] * 3
        specs += [spec, spec, spec8, spec8, spec8]
        scratch = [pltpu.VMEM((8, 128), F32)]
    return pl.pallas_call(
        functools.partial(_inproj_odd_body, tiles_per_seq=tiles_per_seq,
                          q_scale=PROMPT_Q_SCALE if prompt else ATTN_SCALE),
        out_shape=outs,
        grid=(n // tm,),
        in_specs=[pl.BlockSpec((tm, D_MODEL), lambda i: (i, 0)), _full((1, D_MODEL)),
                  _full(w.shape), _full(wf.shape), _full((1, 128))],
        out_specs=specs,
        scratch_shapes=scratch,
        compiler_params=_cp(1),
        name="inproj_odd",
    )(x, g.reshape(1, D_MODEL), w, wf, bf)


def _outproj_body(x_ref, a_ref, b_ref, wa_ref, wb_ref, o_ref):
    o_ref[...] = (x_ref[...]
                  + jnp.dot(a_ref[...].astype(BF), wa_ref[...], preferred_element_type=F32)
                  + jnp.dot(b_ref[...].astype(BF), wb_ref[...], preferred_element_type=F32))


def _outproj(x, a, b, wa, wb):
    n = x.shape[0]
    tm = _row_tile(n, 1024)
    W = GROUP_WIDTH
    return pl.pallas_call(
        _outproj_body,
        out_shape=jax.ShapeDtypeStruct((n, D_MODEL), F32),
        grid=(n // tm,),
        in_specs=[pl.BlockSpec((tm, D_MODEL), lambda i: (i, 0)),
                  pl.BlockSpec((tm, W), lambda i: (i, 0)), pl.BlockSpec((tm, W), lambda i: (i, 0)),
                  _full(wa.shape), _full(wb.shape)],
        out_specs=pl.BlockSpec((tm, D_MODEL), lambda i: (i, 0)),
        compiler_params=_cp(1),
        name="outproj",
    )(x, a, b, wa, wb)


ATTN_QB = 256
ATTN_TK = 256
ATTN_SUB = 64


def _attn_body(lam_ref, qT_ref, k_ref, vT_ref, ck_ref, sel_ref, g_ref, o_ref, s_scr, mx_scr, p_scr, acc_scr,
               *, fox, tq, out_scale):
    qi = pl.program_id(2)
    n2 = 2 * ATTN_QB
    rowd = lax.broadcasted_iota(jnp.int32, (128, ATTN_QB), 0)
    lo = rowd < DH
    koq = (lax.broadcasted_iota(jnp.int32, (ATTN_SUB, n2), 0)
           - (lax.broadcasted_iota(jnp.int32, (ATTN_SUB, n2), 1) & (ATTN_QB - 1)))
    nsub = ATTN_TK // ATTN_SUB

    def query_block(h, _):
        q0 = qi * tq + h * ATTN_QB
        col0 = pl.multiple_of(h * ATTN_QB, ATTN_QB)
        qT = qT_ref[0, :, pl.ds(col0, ATTN_QB)]
        zero = jnp.zeros_like(qT)
        q2 = jnp.concatenate([jnp.where(lo, qT, zero), jnp.where(lo, zero, qT)], axis=1)
        acc_scr[...] = jnp.zeros_like(acc_scr)

        def score_stage(c):
            buf = c & 1
            start = pl.multiple_of(c * ATTN_TK, ATTN_TK)
            mx = jnp.full((ATTN_SUB, n2), NEG, F32)
            for sb in range(nsub):
                r0 = start + sb * ATTN_SUB
                s = jnp.dot(k_ref[0, pl.ds(r0, ATTN_SUB), :], q2, preferred_element_type=F32)
                if fox:
                    s = s + jnp.dot(ck_ref[0, 0, pl.ds(r0, ATTN_SUB), :], sel_ref[...],
                                    preferred_element_type=F32)
                s = jnp.where(koq <= q0 - r0, s, NEG)
                s_scr[buf, sb * ATTN_SUB:(sb + 1) * ATTN_SUB, :] = s
                mx = jnp.maximum(mx, s)
            mx_scr[buf] = mx

        def softmax_stage(c, carry):
            buf = c & 1
            m, l = carry
            mn = jnp.maximum(m, jnp.max(mx_scr[buf], axis=0, keepdims=True))
            al = jnp.exp2(m - mn)
            psum = jnp.zeros((ATTN_SUB, n2), F32)
            for sb in range(nsub):
                p = jnp.exp2(s_scr[buf, sb * ATTN_SUB:(sb + 1) * ATTN_SUB, :] - mn)
                psum = psum + p
                p_scr[sb * ATTN_SUB:(sb + 1) * ATTN_SUB, :] = p.astype(BF)
            l = al * l + jnp.sum(psum, axis=0, keepdims=True)
            start = pl.multiple_of(c * ATTN_TK, ATTN_TK)
            vT = vT_ref[0, :, pl.ds(start, ATTN_TK)]
            acc_scr[...] = al * acc_scr[...] + jnp.dot(vT, p_scr[...], preferred_element_type=F32)
            return mn, l

        def step(c, carry):
            carry = softmax_stage(c, carry)
            score_stage(c + 1)
            return carry

        n_last = q0 // ATTN_TK
        score_stage(0)
        carry = (jnp.full((1, n2), NEG, F32), jnp.zeros((1, n2), F32))
        carry = lax.fori_loop(0, n_last, step, carry)
        m, l = softmax_stage(n_last, carry)
        oT = acc_scr[...] / l
        oa, ob = oT[:, :ATTN_QB], oT[:, ATTN_QB:]
        if fox:
            o = jnp.where(lo, oa, ob).T
        else:
            o = _rms((oa - lam_ref[0] * ob).T, g_ref[...]) * out_scale
        o_ref[0, pl.ds(col0, ATTN_QB), :] = o
        return 0

    lax.fori_loop(0, tq // ATTN_QB, query_block, 0)


def _attn_prompt(qT, k, vT, lam, ck, sel, g, *, fox, out_scale):
    B, T, _ = k.shape
    assert T % ATTN_TK == 0
    tq = 512 if T % 512 == 0 else ATTN_TK
    npair = GROUP_WIDTH // 128
    n2 = 2 * ATTN_QB
    return pl.pallas_call(
        functools.partial(_attn_body, fox=fox, tq=tq, out_scale=out_scale),
        out_shape=jax.ShapeDtypeStruct((B, T, GROUP_WIDTH), F32),
        grid=(B, npair, T // tq),
        in_specs=[pl.BlockSpec(memory_space=pltpu.SMEM),
                  pl.BlockSpec((1, 128, tq), lambda b, j, i: (b, j, i)),
                  pl.BlockSpec((1, T, 128), lambda b, j, i: (b, 0, j)),
                  pl.BlockSpec((1, 128, T), lambda b, j, i: (b, j, 0)),
                  pl.BlockSpec((1, 1, T, 16), lambda b, j, i: (b, j, 0, 0)) if fox
                  else pl.BlockSpec((1, 1, 16, 16), lambda b, j, i: (0, 0, 0, 0)),
                  pl.BlockSpec((16, n2), lambda b, j, i: (0, 0)),
                  pl.BlockSpec((1, 128), lambda b, j, i: (0, 0))],
        out_specs=pl.BlockSpec((1, tq, 128), lambda b, j, i: (b, i, j)),
        scratch_shapes=[pltpu.VMEM((2, ATTN_TK, n2), F32), pltpu.VMEM((2, ATTN_SUB, n2), F32),
                        pltpu.VMEM((ATTN_TK, n2), BF), pltpu.VMEM((128, n2), F32)],
        compiler_params=_cp(3),
        name="attn_fox" if fox else "attn_diff",
    )(lam, qT, k, vT, ck, sel, g)


def _online_update(scs, m, l):
    mx = functools.reduce(jnp.maximum, scs)
    mn = jnp.maximum(m, jnp.max(mx, axis=-1, keepdims=True))
    al = jnp.exp(m - mn)
    ps = [jnp.exp(sc - mn) for sc in scs]
    l = al * l + jnp.sum(functools.reduce(jnp.add, ps), axis=-1, keepdims=True)
    return mn, l, al, ps


_NT = (((1,), (1,)), ((), ()))


def _page2d(ref):
    h, d, r = ref.shape
    return ref[...].reshape(h * d, r)


def _fox_decode_body(pt_ref, q_ref, qbd_ref, kn_ref, vbd_ref, lfn_ref, tri_ref, *rest, G):
    k_refs, v_refs, lf_refs = rest[0:G], rest[G:2 * G], rest[2 * G:3 * G]
    o_ref = rest[3 * G]
    m_scr, l_scr, acc_scr, car_scr = rest[3 * G + 1:]
    s = pl.program_id(1)

    @pl.when(s == 0)
    def _():
        m_scr[...] = jnp.sum(q_ref[0] * kn_ref[0], axis=-1, keepdims=True)
        l_scr[...] = jnp.ones_like(l_scr)
        acc_scr[...] = vbd_ref[0]
        car_scr[...] = lfn_ref[0]

    H = N_FOX_HEADS
    qbd = qbd_ref[0].astype(BF)
    qk = [jnp.dot(qbd, _page2d(k_refs[g]).astype(BF), preferred_element_type=F32)
          for g in range(G)]
    lfT = jnp.concatenate([lf_refs[g][...] for g in range(G)], axis=0)
    bias = _dot3(_split3(lfT), tri_ref[...])
    car = car_scr[...]
    scs = []
    for g in range(G):
        scs.append(qk[g] + (bias[g * H:(g + 1) * H] + car))
        car = car + jnp.sum(lf_refs[g][...], axis=-1, keepdims=True)
    m, l, al, ps = _online_update(scs, m_scr[...], l_scr[...])
    pv = [lax.dot_general(ps[g].astype(BF), _page2d(v_refs[g]).astype(BF), _NT, preferred_element_type=F32)
          for g in range(G)]
    acc = al * acc_scr[...] + functools.reduce(jnp.add, pv)
    m_scr[...], l_scr[...], acc_scr[...], car_scr[...] = m, l, acc, car

    @pl.when(s == pl.num_programs(1) - 1)
    def _():
        own = (lax.broadcasted_iota(jnp.int32, acc.shape, 1) // DH
               == lax.broadcasted_iota(jnp.int32, acc.shape, 0))
        a = jnp.where(own, acc, 0.0)
        o_ref[0] = functools.reduce(jnp.add, [a[:, h * DH:(h + 1) * DH] for h in range(H)]) / l


def _diff_decode_body(pt_ref, lam_ref, q_ref, qbd_ref, kn_ref, vn_ref, g_ref, *rest, G, out_scale):
    k_refs, v_refs = rest[0:G], rest[G:2 * G]
    o_ref = rest[2 * G]
    m_scr, l_scr, acc_scr = rest[2 * G + 1:]
    s = pl.program_id(1)
    nh = 4

    @pl.when(s == 0)
    def _():
        m_scr[...] = jnp.sum(q_ref[0] * kn_ref[0], axis=-1, keepdims=True)
        l_scr[...] = jnp.ones_like(l_scr)
        acc_scr[...] = vn_ref[0]

    qbd = qbd_ref[0].astype(BF)
    scs = [jnp.dot(qbd, _page2d(k_refs[g]).astype(BF), preferred_element_type=F32)
           for g in range(G)]
    m, l, al, ps = _online_update(scs, m_scr[...], l_scr[...])
    page_rows = v_refs[0].shape[0] // nh
    head_of_row = lax.broadcasted_iota(jnp.int32, acc_scr.shape, 0) & (nh - 1)
    upd = jnp.zeros(acc_scr.shape, F32)
    for g in range(G):
        pb = ps[g].astype(BF)
        for h in range(nh):
            vh = v_refs[g][pl.ds(h, page_rows, stride=nh), :].astype(BF)
            upd = upd + jnp.where(head_of_row == h, jnp.dot(pb, vh, preferred_element_type=F32), 0.0)
    acc = al * acc_scr[...] + upd
    m_scr[...], l_scr[...], acc_scr[...] = m, l, acc

    @pl.when(s == pl.num_programs(1) - 1)
    def _():
        o = acc / l
        o = o[0:nh, :] - lam_ref[0] * o[nh:2 * nh, :]
        o_ref[0] = _rms(o, g_ref[...]) * out_scale


def _decode_scratch(width):
    return [pltpu.VMEM((8, 1), F32), pltpu.VMEM((8, 1), F32), pltpu.VMEM((8, width), F32)]


def _spread_rows(a, place):
    B, R, D = a.shape
    return jnp.einsum('brd,rj->brjd', a, place).reshape(B, R, place.shape[1] * D)


def _fox_decode(page_table, layer, q, kn, vn, lfn, cache_kT, cache_vT, cache_lfT):
    B, n_pages = page_table.shape
    G = math.gcd(PAGES_PER_STEP, n_pages)
    H = N_FOX_HEADS
    page_rows = cache_kT.shape[-1]
    eye = jnp.eye(H, dtype=F32)
    tri = (lax.broadcasted_iota(jnp.int32, (page_rows, page_rows), 0)
           > lax.broadcasted_iota(jnp.int32, (page_rows, page_rows), 1)).astype(BF)

    def page(g, nd):
        return lambda b, s, pt: (layer, pt[b, n_pages - 1 - (s * G + g)]) + (0,) * nd

    tok = lambda w: pl.BlockSpec((1, 8, w), lambda b, s, pt: (b, 0, 0))
    kv_block = (None, None, H, DH, page_rows)
    in_specs = ([tok(DH), tok(H * DH), tok(DH), tok(H * DH), tok(1),
                 pl.BlockSpec(tri.shape, lambda b, s, pt: (0, 0))]
                + [pl.BlockSpec(kv_block, page(g, 3)) for g in range(G)]
                + [pl.BlockSpec(kv_block, page(g, 3)) for g in range(G)]
                + [pl.BlockSpec((None, None, H, page_rows), page(g, 2)) for g in range(G)])
    return pl.pallas_call(
        functools.partial(_fox_decode_body, G=G),
        out_shape=jax.ShapeDtypeStruct((B, 8, DH), F32),
        grid_spec=pltpu.PrefetchScalarGridSpec(
            num_scalar_prefetch=1, grid=(B, n_pages // G), in_specs=in_specs,
            out_specs=tok(DH), scratch_shapes=_decode_scratch(H * DH) + [pltpu.VMEM((8, 1), F32)]),
        compiler_params=_cp(2),
        name="fox_decode",
    )(page_table, q, _spread_rows(q, eye), kn, _spread_rows(vn, eye), lfn, tri,
      *([cache_kT] * G), *([cache_vT] * G), *([cache_lfT] * G))


def _diff_decode(page_table, layer, lam, q, kn, vn, g, cache_kT, cache_v, out_scale):
    B, n_pages = page_table.shape
    G = math.gcd(PAGES_PER_STEP, n_pages)
    vrows = cache_v.shape[2]
    nh = 4
    r = jnp.arange(2 * nh)
    place = (jnp.arange(2 * nh)[None, :] == ((r % nh) * 2 + r // nh)[:, None]).astype(F32)

    def page(g_, nd):
        return lambda b, s, pt: (layer, pt[b, s * G + g_]) + (0,) * nd

    tok = lambda rr, w: pl.BlockSpec((1, rr, w), lambda b, s, pt: (b, 0, 0))
    in_specs = ([pl.BlockSpec(memory_space=pltpu.SMEM), tok(8, DH), tok(8, 8 * DH), tok(8, DH), tok(8, 128),
                 pl.BlockSpec((1, 128), lambda b, s, pt: (0, 0))]
                + [pl.BlockSpec((None, None) + cache_kT.shape[2:], page(g_, 3)) for g_ in range(G)]
                + [pl.BlockSpec((None, None, vrows, 128), page(g_, 2)) for g_ in range(G)])
    return pl.pallas_call(
        functools.partial(_diff_decode_body, G=G, out_scale=out_scale),
        out_shape=jax.ShapeDtypeStruct((B, 4, 128), F32),
        grid_spec=pltpu.PrefetchScalarGridSpec(
            num_scalar_prefetch=1, grid=(B, n_pages // G), in_specs=in_specs,
            out_specs=tok(4, 128), scratch_shapes=_decode_scratch(128)),
        compiler_params=_cp(2),
        name="diff_decode",
    )(page_table, lam, q, _spread_rows(q, place), kn, vn, g, *([cache_kT] * G), *([cache_v] * G))


def _s5_body(u_ref, h0r_ref, h0i_ref, ar_ref, ai_ref, bm_ref, cr_ref, ci_ref, d_ref, wg_ref,
             o_ref, hr_out, hi_out, br_scr, bi_scr, hr_scr, hi_scr, *, tc, nb):
    i = pl.program_id(0)
    LB = S5_LANES // S5_BLOCKS
    UB = GROUP_WIDTH // S5_BLOCKS

    @pl.when(i == 0)
    def _():
        hr_scr[...] = h0r_ref[...]
        hi_scr[...] = h0i_ref[...]

    u = u_ref[...]
    ys = []
    for b in range(S5_BLOCKS):
        sl = slice(b * LB, (b + 1) * LB)
        bb = jnp.dot(u[:, b * UB:(b + 1) * UB].astype(BF), bm_ref[b], preferred_element_type=F32)
        br_scr[:, sl] = bb[:, :LB]
        bi_scr[:, sl] = bb[:, LB:]
        ar = jnp.broadcast_to(ar_ref[:, sl], (nb, LB))
        ai = jnp.broadcast_to(ai_ref[:, sl], (nb, LB))

        def step(t, h, sl=sl, ar=ar, ai=ai):
            hr, hi = h
            r0 = pl.multiple_of(t * nb, nb)
            nr = ar * hr - ai * hi + br_scr[pl.ds(r0, nb), sl]
            ni = ar * hi + ai * hr + bi_scr[pl.ds(r0, nb), sl]
            br_scr[pl.ds(r0, nb), sl] = nr
            bi_scr[pl.ds(r0, nb), sl] = ni
            return nr, ni

        hr, hi = lax.fori_loop(0, tc, step, (hr_scr[:, sl], hi_scr[:, sl]))
        hr_scr[:, sl] = hr
        hi_scr[:, sl] = hi
        ys.append(jnp.dot(br_scr[:, sl].astype(BF), cr_ref[b], preferred_element_type=F32)
                  + jnp.dot(bi_scr[:, sl].astype(BF), ci_ref[b], preferred_element_type=F32))
    y = jnp.concatenate(ys, axis=-1) + d_ref[...] * u
    g = jax.nn.gelu(y)
    o_ref[...] = g * jax.nn.sigmoid(jnp.dot(g.astype(BF), wg_ref[...], preferred_element_type=F32))

    @pl.when(i == pl.num_programs(0) - 1)
    def _():
        hr_out[...] = hr_scr[...]
        hi_out[...] = hi_scr[...]


def _s5(u_t, h0r, h0i, p, T, nb):
    tc = min(T, 64)
    assert T % tc == 0
    rows = tc * nb
    st = jax.ShapeDtypeStruct((nb, S5_LANES), F32)
    return pl.pallas_call(
        functools.partial(_s5_body, tc=tc, nb=nb),
        out_shape=[jax.ShapeDtypeStruct((T * nb, GROUP_WIDTH), F32), st, st],
        grid=(T // tc,),
        in_specs=[pl.BlockSpec((rows, GROUP_WIDTH), lambda i: (i, 0)),
                  _full((nb, S5_LANES)), _full((nb, S5_LANES)),
                  _full((1, S5_LANES)), _full((1, S5_LANES)),
                  _full(p["bm"].shape), _full(p["cr"].shape), _full(p["ci"].shape),
                  _full((1, GROUP_WIDTH)), _full(p["wglu"].shape)],
        out_specs=[pl.BlockSpec((rows, GROUP_WIDTH), lambda i: (i, 0)),
                   pl.BlockSpec((nb, S5_LANES), lambda i: (0, 0)),
                   pl.BlockSpec((nb, S5_LANES), lambda i: (0, 0))],
        scratch_shapes=[pltpu.VMEM((rows, S5_LANES), F32), pltpu.VMEM((rows, S5_LANES), F32),
                        pltpu.VMEM((nb, S5_LANES), F32), pltpu.VMEM((nb, S5_LANES), F32)],
        compiler_params=_cp(1),
        name="s5",
    )(u_t, h0r, h0i, p["ar"], p["ai"], p["bm"], p["cr"], p["ci"], p["d"], p["wglu"])


def _lru_body(xb_ref, gate_ref, cv0_ref, h0_ref, cw_ref, cb_ref, wa_ref, ba_ref, wx_ref, bx_ref, nsp_ref,
              o_ref, cv_out, h_out, xp_scr, a_scr, b_scr, h_scr, *, tc, nb):
    i = pl.program_id(0)
    rows = tc * nb
    tail = (CONV_W - 1) * nb

    @pl.when(i == 0)
    def _():
        xp_scr[0:tail, :] = cv0_ref[...]
        h_scr[...] = h0_ref[...]

    @pl.when(i > 0)
    def _():
        xp_scr[0:tail, :] = xp_scr[rows:rows + tail, :]

    xp_scr[tail:tail + rows, :] = xb_ref[...]
    xc = cb_ref[...] + cw_ref[0:1, :] * xp_scr[0:rows, :]
    for j in range(1, CONV_W):
        xc = xc + cw_ref[j:j + 1, :] * xp_scr[j * nb:j * nb + rows, :]
    xcb = xc.astype(BF)
    r = jax.nn.sigmoid(jnp.dot(xcb, wa_ref[...], preferred_element_type=F32) + ba_ref[...])
    ig = jax.nn.sigmoid(jnp.dot(xcb, wx_ref[...], preferred_element_type=F32) + bx_ref[...])
    log_a = nsp_ref[...] * r
    a_scr[...] = jnp.exp(log_a)
    th = jnp.tanh(log_a)
    b_scr[...] = jnp.sqrt(-2.0 * th / (1.0 - th)) * (ig * xc)

    def step(t, h):
        r0 = pl.multiple_of(t * nb, nb)
        h = a_scr[pl.ds(r0, nb), :] * h + b_scr[pl.ds(r0, nb), :]
        b_scr[pl.ds(r0, nb), :] = h
        return h

    h = lax.fori_loop(0, tc, step, h_scr[...])
    h_scr[...] = h
    o_ref[...] = jax.nn.gelu(gate_ref[...]) * b_scr[...]

    @pl.when(i == pl.num_programs(0) - 1)
    def _():
        cv_out[...] = xp_scr[rows:rows + tail, :]
        h_out[...] = h


def _lru(xb_t, gate_t, cv0, h0, p, T, nb):
    tc = min(T, 128)
    assert T % tc == 0
    rows = tc * nb
    W = GROUP_WIDTH
    tail = (CONV_W - 1) * nb
    row_spec = pl.BlockSpec((rows, W), lambda i: (i, 0))
    vec = _full((1, W))
    return pl.pallas_call(
        functools.partial(_lru_body, tc=tc, nb=nb),
        out_shape=[jax.ShapeDtypeStruct((T * nb, W), F32), jax.ShapeDtypeStruct((tail, W), F32),
                   jax.ShapeDtypeStruct((nb, W), F32)],
        grid=(T // tc,),
        in_specs=[row_spec, row_spec, _full((tail, W)), _full((nb, W)), _full((CONV_W, W)), vec,
                  _full((W, W)), vec, _full((W, W)), vec, vec],
        out_specs=[row_spec, pl.BlockSpec((tail, W), lambda i: (0, 0)), pl.BlockSpec((nb, W), lambda i: (0, 0))],
        scratch_shapes=[pltpu.VMEM((rows + tail, W), F32), pltpu.VMEM((rows, W), F32),
                        pltpu.VMEM((rows, W), F32), pltpu.VMEM((nb, W), F32)],
        compiler_params=_cp(1),
        name="rglru",
    )(xb_t, gate_t, cv0, h0, p["cw"], p["cb"], p["wa"], p["ba"], p["wx"], p["bx"], p["nsp"])


def _prep_ffn(wg, wu, wd):
    d, f = wg.shape
    nch = f // FF_CHUNK
    cols = lambda w: w.astype(BF).reshape(d, nch, FF_CHUNK).transpose(1, 0, 2)
    return cols(wg), cols(wu), wd.astype(BF).reshape(nch, FF_CHUNK, d)


def _prep_s5(a_re, a_im, b_re, b_im, c_re, c_im, d_skip, log_dt, w_glu):
    dt = jnp.exp(log_dt)[:, None]
    mag = jnp.exp(a_re * dt)
    ab_re, ab_im = mag * jnp.cos(a_im * dt), mag * jnp.sin(a_im * dt)
    den = a_re * a_re + a_im * a_im
    nr, ni = ab_re - 1.0, ab_im
    co_re, co_im = (nr * a_re + ni * a_im) / den, (ni * a_re - nr * a_im) / den
    bt_re = co_re[..., None] * b_re - co_im[..., None] * b_im
    bt_im = co_re[..., None] * b_im + co_im[..., None] * b_re
    gpb = S5_GROUPS // S5_BLOCKS
    eye = jnp.eye(gpb, dtype=F32)

    def blk_in(bt):
        x = bt.reshape(S5_BLOCKS, gpb, S5_STATE, S5_CH)
        return jnp.einsum('bgph,gk->bghkp', x, eye).reshape(S5_BLOCKS, gpb * S5_CH, gpb * S5_STATE)

    def blk_out(c):
        x = c.reshape(S5_BLOCKS, gpb, S5_CH, S5_STATE)
        return jnp.einsum('bghp,gk->bgpkh', x, eye).reshape(S5_BLOCKS, gpb * S5_STATE, gpb * S5_CH)

    return {"ar": ab_re.reshape(1, S5_LANES), "ai": ab_im.reshape(1, S5_LANES),
            "bm": jnp.concatenate([blk_in(bt_re), blk_in(bt_im)], axis=-1).astype(BF),
            "cr": blk_out(c_re).astype(BF), "ci": blk_out(-c_im).astype(BF),
            "d": d_skip.reshape(1, GROUP_WIDTH), "wglu": w_glu.astype(BF)}


def _prep_lru(conv_w, conv_b, w_a, b_a, w_x, b_x, lam):
    eye = jnp.eye(LRU_BLOCKS, dtype=F32)
    W = GROUP_WIDTH
    dense = lambda w: jnp.einsum('nde,nm->ndme', w, eye).reshape(W, W).astype(BF)
    row = lambda v: v.reshape(1, W)
    return {"cw": conv_w, "cb": row(conv_b), "wa": dense(w_a), "ba": row(b_a), "wx": dense(w_x),
            "bx": row(b_x), "nsp": row(-LRU_C * jax.nn.softplus(-lam))}


def _to_time_major(a, B, T):
    return a.reshape(B, T, -1).transpose(1, 0, 2).reshape(T * B, -1)


def _to_batch_major(a, B, T):
    return a.reshape(T, B, -1).transpose(1, 0, 2).reshape(B * T, -1)


def _trunk(x, B, T, P, st):
    prompt = st is None
    W = GROUP_WIDTH
    kd, vd, sr, si, kf, vf, lf, cv, lr = ([] for _ in range(9))
    depth = len(P["ffn"])
    for l in range(depth):
        g = P["norm_g"][l]
        x = _ffn(x, g[0], P["ffn"][l][0])
        if l % 2 == 0:
            e = l // 2
            lam_init = 0.8 - 0.6 * math.exp(-0.3 * l)
            lam = P["lam"][e]
            res = _inproj_even(x, g[1], P["w_in_even"][e], prompt)
            u, q, k, v = res[:4]
            if prompt:
                kb, vb = res[4:]
                tr = lambda a: a.reshape(B, T, W).transpose(0, 2, 1)
                attn = _attn_prompt(tr(q), kb.reshape(B, T, W), tr(vb), lam,
                                    P["dummy_ck"], P["sel"], P["subln_g"][e],
                                    fox=False, out_scale=1.0 - lam_init).reshape(B * T, W)
                h0r = h0i = jnp.zeros((B, S5_LANES), F32)
                u_t = _to_time_major(u, B, T)
            else:
                nh = W // (2 * DH)
                mh = lambda a: a.reshape(B, nh, 2, DH).transpose(0, 2, 1, 3).reshape(B, 2 * nh, DH)
                vn = v.reshape(B, nh, 2 * DH)
                attn = _diff_decode(st["page_table"], e, lam, mh(q), mh(k),
                                    jnp.concatenate([vn, vn], axis=1), P["subln_g"][e],
                                    st["k_diff"], st["v_diff"], 1.0 - lam_init).reshape(B, W)
                h0r = st["s5_re"][e].reshape(B, S5_LANES)
                h0i = st["s5_im"][e].reshape(B, S5_LANES)
                u_t = u
            s5_t, hr, hi = _s5(u_t, h0r, h0i, P["s5"][e], T, B)
            rec = _to_batch_major(s5_t, B, T) if prompt else s5_t
            x = _outproj(x, rec, attn, *P["w_out_even"][e])
            kd.append(k.reshape(B, T, 2 * (W // (2 * DH)), DH))
            vd.append(v.reshape(B, T, W // (2 * DH), 2 * DH))
            sr.append(hr.reshape(B, S5_GROUPS, S5_STATE))
            si.append(hi.reshape(B, S5_GROUPS, S5_STATE))
        else:
            o = l // 2
            H = N_FOX_HEADS
            res = _inproj_odd(x, g[1], P["w_in_odd"][o], P["w_f"][o], P["b_f"][o], prompt, T)
            xb, gate, q, k, v, logf = res[:6]
            if prompt:
                kb, vb = res[6:8]
                tr = lambda a: a.reshape(B, T, W).transpose(0, 2, 1)
                pieces = jnp.stack(res[8:11], axis=-1)
                ck = jnp.pad(pieces.reshape(B, T, H // 2, 6), ((0, 0), (0, 0), (0, 0), (0, 10)))
                attn = _attn_prompt(tr(q), kb.reshape(B, T, W), tr(vb), P["lam"][0],
                                    ck.transpose(0, 2, 1, 3), P["sel"], P["subln_g"][0],
                                    fox=True, out_scale=1.0).reshape(B * T, W)
                cv0 = jnp.zeros(((CONV_W - 1) * B, W), F32)
                h0 = jnp.zeros((B, W), F32)
                xb_t, gate_t = _to_time_major(xb, B, T), _to_time_major(gate, B, T)
            else:
                hd = lambda a: a.reshape(B, H, DH)
                attn = _fox_decode(st["page_table"], o, hd(q), hd(k), hd(v), logf.reshape(B, H, 1),
                                   st["k_fox"], st["v_fox"], st["logfT"]).reshape(B, W)
                cv0 = st["conv"][o].transpose(1, 0, 2).reshape((CONV_W - 1) * B, W)
                h0 = st["lru"][o]
                xb_t, gate_t = xb, gate
            y_t, cv_new, h_new = _lru(xb_t, gate_t, cv0, h0, P["lru"][o], T, B)
            rec = _to_batch_major(y_t, B, T) if prompt else y_t
            x = _outproj(x, rec, attn, *P["w_out_odd"][o])
            kf.append(k.reshape(B, T, H, DH))
            vf.append(v.reshape(B, T, H, DH))
            lf.append(logf.reshape(B, T, H))
            cv.append(cv_new.reshape(CONV_W - 1, B, W).transpose(1, 0, 2))
            lr.append(h_new)
        x = _ffn(x, g[2], P["ffn"][l][1], final_g=P["final_norm_g"] if l == depth - 1 else None)
    return (x.reshape(B, T, D_MODEL), jnp.stack(kd), jnp.stack(vd), jnp.stack(sr), jnp.stack(si),
            jnp.stack(kf), jnp.stack(vf), jnp.stack(lf), jnp.stack(cv), jnp.stack(lr))


def kernel(x_prompt, x_sample, cache_k_diff, cache_v_diff, state_s5_re, state_s5_im, cache_k_fox, cache_v_fox, cache_logf_fox, state_conv, state_lru, page_table, norm_g, final_norm_g, w_ffn_gate, w_ffn_up, w_ffn_down, w_in_even, w_out_even, s5_a_re, s5_a_im, s5_b_re, s5_b_im, s5_c_re, s5_c_im, s5_d, s5_log_dt, s5_w_glu, lambda_q1, lambda_k1, lambda_q2, lambda_k2, diff_subln_g, w_in_odd, w_out_odd, fox_b_f, conv_w, conv_b, lru_w_a, lru_b_a, lru_w_x, lru_b_x, lru_lambda):
    depth = w_ffn_gate.shape[0]
    n_even, n_odd = w_in_even.shape[0], w_in_odd.shape[0]
    W = GROUP_WIDTH
    H = N_FOX_HEADS
    main = 2 * W + 3 * W
    lam = (jnp.exp(jnp.sum(lambda_q1 * lambda_k1, axis=-1)) - jnp.exp(jnp.sum(lambda_q2 * lambda_k2, axis=-1)))
    lam_init = jnp.array([0.8 - 0.6 * math.exp(-0.3 * 2 * e) for e in range(n_even)], F32)
    P = {
        "norm_g": norm_g, "final_norm_g": final_norm_g,
        "ffn": [[_prep_ffn(w_ffn_gate[l, i], w_ffn_up[l, i], w_ffn_down[l, i]) for i in range(2)]
                for l in range(depth)],
        "w_in_even": w_in_even.astype(BF),
        "w_out_even": [(w_out_even[e, :W].astype(BF), w_out_even[e, W:].astype(BF)) for e in range(n_even)],
        "s5": [_prep_s5(s5_a_re[e], s5_a_im[e], s5_b_re[e], s5_b_im[e], s5_c_re[e], s5_c_im[e],
                        s5_d[e], s5_log_dt[e], s5_w_glu[e]) for e in range(n_even)],
        "lam": (lam + lam_init).reshape(n_even, 1),
        "subln_g": diff_subln_g.reshape(n_even, 1, 2 * DH),
        "w_in_odd": w_in_odd[:, :, :main].astype(BF),
        "w_f": jnp.pad(w_in_odd[:, :, main:], ((0, 0), (0, 0), (0, 128 - H))).astype(BF),
        "b_f": jnp.pad(fox_b_f, ((0, 0), (0, 128 - H))).reshape(n_odd, 1, 128),
        "w_out_odd": [(w_out_odd[o, :W].astype(BF), w_out_odd[o, W:].astype(BF)) for o in range(n_odd)],
        "lru": [_prep_lru(conv_w[o], conv_b[o], lru_w_a[o], lru_b_a[o], lru_w_x[o], lru_b_x[o], lru_lambda[o])
                for o in range(n_odd)],
        "dummy_ck": jnp.zeros((1, 1, 16, 16), BF),
        "sel": (lax.broadcasted_iota(jnp.int32, (16, 2 * ATTN_QB), 0) // 3
                == lax.broadcasted_iota(jnp.int32, (16, 2 * ATTN_QB), 1) // ATTN_QB).astype(BF),
    }
    Bp, Tp, _ = x_prompt.shape
    Bs, Ts, _ = x_sample.shape
    n_phys, page = cache_k_diff.shape[1], cache_k_diff.shape[2]
    st = {
        "page_table": page_table,
        "k_diff": cache_k_diff.transpose(0, 1, 3, 4, 2),
        "v_diff": cache_v_diff.reshape(n_even, n_phys, page * (W // (2 * DH)), 2 * DH),
        "k_fox": cache_k_fox.transpose(0, 1, 3, 4, 2),
        "v_fox": cache_v_fox.transpose(0, 1, 3, 4, 2),
        "logfT": cache_logf_fox.transpose(0, 1, 3, 2),
        "s5_re": state_s5_re, "s5_im": state_s5_im, "conv": state_conv, "lru": state_lru,
    }
    outs_p = _trunk(x_prompt.reshape(Bp * Tp, D_MODEL), Bp, Tp, P, None)
    outs_s = _trunk(x_sample.reshape(Bs * Ts, D_MODEL), Bs, Ts, P, st)
    return (outs_p[0], outs_s[0]) + tuple(outs_p[1:]) + tuple(outs_s[1:])
```

```python
import functools
import math

import jax
import jax.numpy as jnp
from jax import lax
from jax.experimental import pallas as pl
from jax.experimental.pallas import tpu as pltpu

F32 = jnp.float32
BF = jnp.bfloat16

D_MODEL = 1024
GROUP_WIDTH = 512
DH = 64
S5_GROUPS = 32
S5_STATE = 64
S5_CH = 16
S5_LANES = S5_GROUPS * S5_STATE
S5_BLOCKS = 4
LRU_BLOCKS = 8
LRU_C = 8.0
CONV_W = 4
N_FOX_HEADS = 8
FF_CHUNK = 256
EPS = 1e-6
NEG = -1e30
ATTN_SCALE = DH ** -0.5
LOG2E = math.log2(math.e)
PROMPT_Q_SCALE = ATTN_SCALE * LOG2E

VMEM_LIMIT_V7X = 56 * 1024 * 1024
PAGES_PER_STEP = 8


def _cp(n_axes, vmem=VMEM_LIMIT_V7X):
    return pltpu.CompilerParams(dimension_semantics=("arbitrary",) * n_axes,
                                vmem_limit_bytes=vmem)


def _rms(x, g):
    return x * lax.rsqrt(jnp.mean(x * x, axis=-1, keepdims=True) + EPS) * g


def _full(shape, single=True):
    nd = len(shape)
    kw = {"pipeline_mode": pl.Buffered(1)} if single else {}
    return pl.BlockSpec(shape, lambda *_: (0,) * nd, **kw)


def _row_tile(n, want):
    t = min(n, want)
    assert n % t == 0
    return t


def _ffn_body(x_ref, g_ref, wg_ref, wu_ref, wd_ref, fg_ref, o_ref, hn_ref, *, n_chunks, final):
    x = x_ref[...]
    hn_ref[...] = _rms(x, g_ref[...]).astype(BF)
    o_ref[...] = x

    def chunk(j, c):
        hn = hn_ref[...]
        a = jnp.dot(hn, wg_ref[j], preferred_element_type=F32)
        b = jnp.dot(hn, wu_ref[j], preferred_element_type=F32)
        h = (a * jax.nn.sigmoid(a) * b).astype(BF)
        o_ref[...] += 0.5 * jnp.dot(h, wd_ref[j], preferred_element_type=F32)
        return c

    lax.fori_loop(0, n_chunks, chunk, 0)
    if final:
        o_ref[...] = _rms(o_ref[...], fg_ref[...])


def _ffn(x, g, w, final_g=None):
    n = x.shape[0]
    wg, wu, wd = w
    nch = wg.shape[0]
    tm = _row_tile(n, 1024)
    fg = g if final_g is None else final_g
    return pl.pallas_call(
        functools.partial(_ffn_body, n_chunks=nch, final=final_g is not None),
        out_shape=jax.ShapeDtypeStruct((n, D_MODEL), F32),
        grid=(n // tm,),
        in_specs=[pl.BlockSpec((tm, D_MODEL), lambda i: (i, 0)),
                  _full((1, D_MODEL)), _full(wg.shape), _full(wu.shape), _full(wd.shape),
                  _full((1, D_MODEL))],
        out_specs=pl.BlockSpec((tm, D_MODEL), lambda i: (i, 0)),
        scratch_shapes=[pltpu.VMEM((tm, D_MODEL), BF)],
        compiler_params=_cp(1),
        name="ffn",
    )(x, g.reshape(1, D_MODEL), wg, wu, wd, fg.reshape(1, D_MODEL))


def _store(ref, val):
    if len(ref.shape) == 3:
        ref[0] = val.T.astype(ref.dtype)
    else:
        ref[...] = val.astype(ref.dtype)


def _proj_out(n, tm, seq_len, dtype, transposed):
    W = GROUP_WIDTH
    if not transposed:
        return jax.ShapeDtypeStruct((n, W), dtype), pl.BlockSpec((tm, W), lambda i: (i, 0))
    tps = seq_len // tm
    return (jax.ShapeDtypeStruct((n // seq_len, W, seq_len), dtype),
            pl.BlockSpec((1, W, tm), lambda i: (i // tps, 0, i % tps)))


def _inproj_even_body(x_ref, g_ref, w_ref, u_ref, q_ref, k_ref, v_ref, *bf_refs, q_scale):
    hn = _rms(x_ref[...], g_ref[...]).astype(BF)
    W = GROUP_WIDTH

    def col(c):
        return jnp.dot(hn, w_ref[:, c * W:(c + 1) * W], preferred_element_type=F32)

    u_ref[...] = col(0)
    _store(q_ref, col(1) * q_scale)
    k = col(2)
    v = col(3)
    _store(k_ref, k)
    _store(v_ref, v)
    if bf_refs:
        _store(bf_refs[0], k)
        _store(bf_refs[1], v)


def _inproj_even(x, g, w, prompt, seq_len):
    n = x.shape[0]
    tm = _row_tile(n, 512)
    out = functools.partial(_proj_out, n, tm, seq_len)
    outs = [out(F32, False), out(BF if prompt else F32, prompt), out(F32, prompt), out(F32, False)]
    if prompt:
        assert seq_len % tm == 0
        outs += [out(BF, False), out(BF, True)]
    return pl.pallas_call(
        functools.partial(_inproj_even_body, q_scale=PROMPT_Q_SCALE if prompt else ATTN_SCALE),
        out_shape=[o[0] for o in outs],
        grid=(n // tm,),
        in_specs=[pl.BlockSpec((tm, D_MODEL), lambda i: (i, 0)), _full((1, D_MODEL)), _full(w.shape)],
        out_specs=[o[1] for o in outs],
        compiler_params=_cp(1),
        name="inproj_even",
    )(x, g.reshape(1, D_MODEL), w)


def _split3(x):
    h = x.astype(BF)
    r = x - h.astype(F32)
    m = r.astype(BF)
    lo = (r - m.astype(F32)).astype(BF)
    return h, m, lo


def _dot3(pieces, w, dims=None):
    h, m, lo = pieces
    if dims is None:
        d = lambda a: jnp.dot(a, w, preferred_element_type=F32)
    else:
        d = lambda a: lax.dot_general(a, w, dims, preferred_element_type=F32)
    return (d(lo) + d(m)) + d(h)


def _log_sigmoid(z):
    return -(jnp.maximum(-z, 0.0) + jnp.log1p(jnp.exp(-jnp.abs(z))))


CUM_BLOCK = 256


def _inproj_odd_body(x_ref, g_ref, w_ref, wf_ref, bf_ref, xb_ref, gate_ref, q_ref, k_ref, v_ref,
                     lf_ref, *rest, tiles_per_seq, q_scale):
    hn = _rms(x_ref[...], g_ref[...]).astype(BF)
    W = GROUP_WIDTH

    def col(c):
        return jnp.dot(hn, w_ref[:, c * W:(c + 1) * W], preferred_element_type=F32)

    xb_ref[...] = col(0)
    gate_ref[...] = col(1)
    _store(q_ref, col(2) * q_scale)
    k = col(3)
    v = col(4)
    _store(k_ref, k)
    _store(v_ref, v)
    f = jnp.dot(hn, wf_ref[...], preferred_element_type=F32) + bf_ref[...]
    lf = _log_sigmoid(f)
    lf_ref[...] = lf[:, :N_FOX_HEADS]
    if rest:
        kb_ref, vb_ref, ch_ref, cm_ref, cl_ref, carry_ref = rest
        _store(kb_ref, k)
        _store(vb_ref, v)

        @pl.when(pl.program_id(0) % tiles_per_seq == 0)
        def _():
            carry_ref[...] = jnp.zeros_like(carry_ref)

        tm = lf.shape[0]
        nb = tm // CUM_BLOCK
        r = lax.broadcasted_iota(jnp.int32, (CUM_BLOCK, CUM_BLOCK), 0)
        c = lax.broadcasted_iota(jnp.int32, (CUM_BLOCK, CUM_BLOCK), 1)
        tril = jnp.where(c <= r, 1.0, 0.0).astype(BF)
        carry = carry_ref[0:1, :]
        for b in range(nb):
            blk = lf[b * CUM_BLOCK:(b + 1) * CUM_BLOCK, :]
            h, m, lo = _split3(blk)
            d = lambda a: jnp.dot(tril, a, preferred_element_type=F32)
            cum = ((d(lo) + d(m)) + d(h)) + carry
            rows = slice(b * CUM_BLOCK, (b + 1) * CUM_BLOCK)
            for piece, ref in zip(_split3(cum * (-LOG2E)), (ch_ref, cm_ref, cl_ref)):
                ref[rows, :] = piece[:, :N_FOX_HEADS]
            carry = cum[CUM_BLOCK - 1:CUM_BLOCK, :]
        carry_ref[0:1, :] = carry


def _inproj_odd(x, g, w, wf, bf, prompt, seq_len):
    n = x.shape[0]
    tm = _row_tile(n, 512)
    H = N_FOX_HEADS
    out = functools.partial(_proj_out, n, tm, seq_len)
    per_head = lambda dt: (jax.ShapeDtypeStruct((n, H), dt), pl.BlockSpec((tm, H), lambda i: (i, 0)))
    outs = [out(F32, False), out(F32, False), out(BF if prompt else F32, prompt), out(F32, prompt),
            out(F32, prompt), per_head(F32)]
    scratch = []
    tiles_per_seq = 1
    if prompt:
        assert seq_len % tm == 0 and tm % CUM_BLOCK == 0
        tiles_per_seq = seq_len // tm
        outs += [out(BF, False), out(BF, True)] + [per_head(BF)] * 3
        scratch = [pltpu.VMEM((8, 128), F32)]
    return pl.pallas_call(
        functools.partial(_inproj_odd_body, tiles_per_seq=tiles_per_seq,
                          q_scale=PROMPT_Q_SCALE if prompt else ATTN_SCALE),
        out_shape=[o[0] for o in outs],
        grid=(n // tm,),
        in_specs=[pl.BlockSpec((tm, D_MODEL), lambda i: (i, 0)), _full((1, D_MODEL)),
                  _full(w.shape), _full(wf.shape), _full((1, 128))],
        out_specs=[o[1] for o in outs],
        scratch_shapes=scratch,
        compiler_params=_cp(1),
        name="inproj_odd",
    )(x, g.reshape(1, D_MODEL), w, wf, bf)


def _outproj_body(x_ref, a_ref, b_ref, wa_ref, wb_ref, o_ref):
    o_ref[...] = (x_ref[...]
                  + jnp.dot(a_ref[...].astype(BF), wa_ref[...], preferred_element_type=F32)
                  + jnp.dot(b_ref[...].astype(BF), wb_ref[...], preferred_element_type=F32))


def _outproj(x, a, b, wa, wb):
    n = x.shape[0]
    tm = _row_tile(n, 1024)
    W = GROUP_WIDTH
    return pl.pallas_call(
        _outproj_body,
        out_shape=jax.ShapeDtypeStruct((n, D_MODEL), F32),
        grid=(n // tm,),
        in_specs=[pl.BlockSpec((tm, D_MODEL), lambda i: (i, 0)),
                  pl.BlockSpec((tm, W), lambda i: (i, 0)), pl.BlockSpec((tm, W), lambda i: (i, 0)),
                  _full(wa.shape), _full(wb.shape)],
        out_specs=pl.BlockSpec((tm, D_MODEL), lambda i: (i, 0)),
        compiler_params=_cp(1),
        name="outproj",
    )(x, a, b, wa, wb)


ATTN_QB = 256
ATTN_TK = 512
ATTN_SUB = 64


def _attn_body(lam_ref, qT_ref, k_ref, vT_ref, sel_ref, g_ref, o_ref, s_scr, mx_scr, p_scr, acc_scr,
               *, fox, tq, out_scale):
    qi = pl.program_id(2)
    n2 = 2 * ATTN_QB
    rowd = lax.broadcasted_iota(jnp.int32, (128, ATTN_QB), 0)
    lo = rowd < DH
    koq = (lax.broadcasted_iota(jnp.int32, (ATTN_SUB, n2), 0)
           - (lax.broadcasted_iota(jnp.int32, (ATTN_SUB, n2), 1) & (ATTN_QB - 1)))
    nsub = ATTN_TK // ATTN_SUB

    def query_block(h, _):
        q0 = qi * tq + h * ATTN_QB
        col0 = pl.multiple_of(h * ATTN_QB, ATTN_QB)
        qT = qT_ref[0, :, pl.ds(col0, ATTN_QB)]
        zero = jnp.zeros_like(qT)
        q2 = jnp.concatenate([jnp.where(lo, qT, zero), jnp.where(lo, zero, qT)], axis=1)
        if fox:
            q2 = jnp.concatenate([q2, sel_ref[...]], axis=0)
        acc_scr[...] = jnp.zeros_like(acc_scr)

        def score_stage(c):
            buf = c & 1
            start = pl.multiple_of(c * ATTN_TK, ATTN_TK)
            mx = jnp.full((ATTN_SUB, n2), NEG, F32)
            for sb in range(nsub):
                r0 = start + sb * ATTN_SUB
                krows = k_ref[0, 0, pl.ds(r0, ATTN_SUB), :] if fox else k_ref[0, pl.ds(r0, ATTN_SUB), :]
                s = jnp.dot(krows, q2, preferred_element_type=F32)
                s = jnp.where(koq <= q0 - r0, s, NEG)
                s_scr[buf, sb * ATTN_SUB:(sb + 1) * ATTN_SUB, :] = s
                mx = jnp.maximum(mx, s)
            mx_scr[buf] = mx

        def softmax_stage(c, carry):
            buf = c & 1
            m, l = carry
            mn = jnp.maximum(m, jnp.max(mx_scr[buf], axis=0, keepdims=True))
            al = jnp.exp2(m - mn)
            psum = jnp.zeros((ATTN_SUB, n2), F32)
            for sb in range(nsub):
                p = jnp.exp2(s_scr[buf, sb * ATTN_SUB:(sb + 1) * ATTN_SUB, :] - mn)
                psum = psum + p
                p_scr[sb * ATTN_SUB:(sb + 1) * ATTN_SUB, :] = p.astype(BF)
            l = al * l + jnp.sum(psum, axis=0, keepdims=True)
            start = pl.multiple_of(c * ATTN_TK, ATTN_TK)
            vT = vT_ref[0, :, pl.ds(start, ATTN_TK)]
            acc_scr[...] = al * acc_scr[...] + jnp.dot(vT, p_scr[...], preferred_element_type=F32)
            return mn, l

        def step(c, carry):
            carry = softmax_stage(c, carry)
            score_stage(c + 1)
            return carry

        n_last = q0 // ATTN_TK
        score_stage(0)
        carry = (jnp.full((1, n2), NEG, F32), jnp.zeros((1, n2), F32))
        carry = lax.fori_loop(0, n_last, step, carry)
        m, l = softmax_stage(n_last, carry)
        oT = acc_scr[...] / l
        oa, ob = oT[:, :ATTN_QB], oT[:, ATTN_QB:]
        if fox:
            o = jnp.where(lo, oa, ob).T
        else:
            o = _rms((oa - lam_ref[0] * ob).T, g_ref[...]) * out_scale
        o_ref[0, pl.ds(col0, ATTN_QB), :] = o
        return 0

    lax.fori_loop(0, tq // ATTN_QB, query_block, 0)


def _attn_prompt(qT, k, vT, lam, sel, g, *, fox, out_scale):
    B, _, T = qT.shape
    assert T % ATTN_TK == 0
    tq = 512 if T % 512 == 0 else ATTN_TK
    npair = GROUP_WIDTH // 128
    n2 = 2 * ATTN_QB
    return pl.pallas_call(
        functools.partial(_attn_body, fox=fox, tq=tq, out_scale=out_scale),
        out_shape=jax.ShapeDtypeStruct((B, T, GROUP_WIDTH), F32),
        grid=(B, npair, T // tq),
        in_specs=[pl.BlockSpec(memory_space=pltpu.SMEM),
                  pl.BlockSpec((1, 128, tq), lambda b, j, i: (b, j, i)),
                  pl.BlockSpec((1, 1, T, k.shape[-1]), lambda b, j, i: (b, j, 0, 0)) if fox
                  else pl.BlockSpec((1, T, 128), lambda b, j, i: (b, 0, j)),
                  pl.BlockSpec((1, 128, T), lambda b, j, i: (b, j, 0)),
                  pl.BlockSpec((16, n2), lambda b, j, i: (0, 0)),
                  pl.BlockSpec((1, 128), lambda b, j, i: (0, 0))],
        out_specs=pl.BlockSpec((1, tq, 128), lambda b, j, i: (b, i, j)),
        scratch_shapes=[pltpu.VMEM((2, ATTN_TK, n2), F32), pltpu.VMEM((2, ATTN_SUB, n2), F32),
                        pltpu.VMEM((ATTN_TK, n2), BF), pltpu.VMEM((128, n2), F32)],
        compiler_params=_cp(3),
        name="attn_fox" if fox else "attn_diff",
    )(lam, qT, k, vT, sel, g)


def _online_update(scs, m, l):
    mx = functools.reduce(jnp.maximum, scs)
    mn = jnp.maximum(m, jnp.max(mx, axis=-1, keepdims=True))
    al = jnp.exp(m - mn)
    ps = [jnp.exp(sc - mn) for sc in scs]
    l = al * l + jnp.sum(functools.reduce(jnp.add, ps), axis=-1, keepdims=True)
    return mn, l, al, ps


_NT = (((1,), (1,)), ((), ()))


def _page2d(ref):
    h, d, r = ref.shape
    return ref[...].reshape(h * d, r)


def _fox_decode_body(pt_ref, q_ref, qbd_ref, kn_ref, vbd_ref, lfn_ref, tri_ref, *rest, G):
    k_refs, v_refs, lf_refs = rest[0:G], rest[G:2 * G], rest[2 * G:3 * G]
    o_ref = rest[3 * G]
    m_scr, l_scr, acc_scr, car_scr = rest[3 * G + 1:]
    s = pl.program_id(1)

    @pl.when(s == 0)
    def _():
        m_scr[...] = jnp.sum(q_ref[0] * kn_ref[0], axis=-1, keepdims=True)
        l_scr[...] = jnp.ones_like(l_scr)
        acc_scr[...] = vbd_ref[0]
        car_scr[...] = lfn_ref[0]

    H = N_FOX_HEADS
    qbd = qbd_ref[0].astype(BF)
    qk = [jnp.dot(qbd, _page2d(k_refs[g]).astype(BF), preferred_element_type=F32)
          for g in range(G)]
    lfT = jnp.concatenate([lf_refs[g][...] for g in range(G)], axis=0)
    bias = _dot3(_split3(lfT), tri_ref[...])
    car = car_scr[...]
    scs = []
    for g in range(G):
        scs.append(qk[g] + (bias[g * H:(g + 1) * H] + car))
        car = car + jnp.sum(lf_refs[g][...], axis=-1, keepdims=True)
    m, l, al, ps = _online_update(scs, m_scr[...], l_scr[...])
    pv = [lax.dot_general(ps[g].astype(BF), _page2d(v_refs[g]).astype(BF), _NT, preferred_element_type=F32)
          for g in range(G)]
    acc = al * acc_scr[...] + functools.reduce(jnp.add, pv)
    m_scr[...], l_scr[...], acc_scr[...], car_scr[...] = m, l, acc, car

    @pl.when(s == pl.num_programs(1) - 1)
    def _():
        own = (lax.broadcasted_iota(jnp.int32, acc.shape, 1) // DH
               == lax.broadcasted_iota(jnp.int32, acc.shape, 0))
        a = jnp.where(own, acc, 0.0)
        o_ref[0] = functools.reduce(jnp.add, [a[:, h * DH:(h + 1) * DH] for h in range(H)]) / l


def _diff_decode_body(pt_ref, lam_ref, q_ref, qbd_ref, kn_ref, vn_ref, g_ref, *rest, G, out_scale):
    k_refs, v_refs = rest[0:G], rest[G:2 * G]
    o_ref = rest[2 * G]
    m_scr, l_scr, acc_scr = rest[2 * G + 1:]
    s = pl.program_id(1)
    nh = 4

    @pl.when(s == 0)
    def _():
        m_scr[...] = jnp.sum(q_ref[0] * kn_ref[0], axis=-1, keepdims=True)
        l_scr[...] = jnp.ones_like(l_scr)
        acc_scr[...] = vn_ref[0]

    qbd = qbd_ref[0].astype(BF)
    scs = [jnp.dot(qbd, _page2d(k_refs[g]).astype(BF), preferred_element_type=F32)
           for g in range(G)]
    m, l, al, ps = _online_update(scs, m_scr[...], l_scr[...])
    page_rows = v_refs[0].shape[0] // nh
    head_of_row = lax.broadcasted_iota(jnp.int32, acc_scr.shape, 0) & (nh - 1)
    upd = jnp.zeros(acc_scr.shape, F32)
    for g in range(G):
        pb = ps[g].astype(BF)
        for h in range(nh):
            vh = v_refs[g][pl.ds(h, page_rows, stride=nh), :].astype(BF)
            upd = upd + jnp.where(head_of_row == h, jnp.dot(pb, vh, preferred_element_type=F32), 0.0)
    acc = al * acc_scr[...] + upd
    m_scr[...], l_scr[...], acc_scr[...] = m, l, acc

    @pl.when(s == pl.num_programs(1) - 1)
    def _():
        o = acc / l
        o = o[0:nh, :] - lam_ref[0] * o[nh:2 * nh, :]
        o_ref[0] = _rms(o, g_ref[...]) * out_scale


def _decode_scratch(width):
    return [pltpu.VMEM((8, 1), F32), pltpu.VMEM((8, 1), F32), pltpu.VMEM((8, width), F32)]


def _spread_rows(a, place):
    B, R, D = a.shape
    return jnp.einsum('brd,rj->brjd', a, place).reshape(B, R, place.shape[1] * D)


def _fox_decode(page_table, layer, q, kn, vn, lfn, cache_kT, cache_vT, cache_lfT):
    B, n_pages = page_table.shape
    G = math.gcd(PAGES_PER_STEP, n_pages)
    H = N_FOX_HEADS
    page_rows = cache_kT.shape[-1]
    eye = jnp.eye(H, dtype=F32)
    tri = (lax.broadcasted_iota(jnp.int32, (page_rows, page_rows), 0)
           > lax.broadcasted_iota(jnp.int32, (page_rows, page_rows), 1)).astype(BF)

    def page(g, nd):
        return lambda b, s, pt: (layer, pt[b, n_pages - 1 - (s * G + g)]) + (0,) * nd

    tok = lambda w: pl.BlockSpec((1, 8, w), lambda b, s, pt: (b, 0, 0))
    kv_block = (None, None, H, DH, page_rows)
    in_specs = ([tok(DH), tok(H * DH), tok(DH), tok(H * DH), tok(1),
                 pl.BlockSpec(tri.shape, lambda b, s, pt: (0, 0))]
                + [pl.BlockSpec(kv_block, page(g, 3)) for g in range(G)]
                + [pl.BlockSpec(kv_block, page(g, 3)) for g in range(G)]
                + [pl.BlockSpec((None, None, H, page_rows), page(g, 2)) for g in range(G)])
    return pl.pallas_call(
        functools.partial(_fox_decode_body, G=G),
        out_shape=jax.ShapeDtypeStruct((B, 8, DH), F32),
        grid_spec=pltpu.PrefetchScalarGridSpec(
            num_scalar_prefetch=1, grid=(B, n_pages // G), in_specs=in_specs,
            out_specs=tok(DH), scratch_shapes=_decode_scratch(H * DH) + [pltpu.VMEM((8, 1), F32)]),
        compiler_params=_cp(2),
        name="fox_decode",
    )(page_table, q, _spread_rows(q, eye), kn, _spread_rows(vn, eye), lfn, tri,
      *([cache_kT] * G), *([cache_vT] * G), *([cache_lfT] * G))


def _diff_decode(page_table, layer, lam, q, kn, vn, g, cache_kT, cache_v, out_scale):
    B, n_pages = page_table.shape
    G = math.gcd(PAGES_PER_STEP, n_pages)
    vrows = cache_v.shape[2]
    nh = 4
    r = jnp.arange(2 * nh)
    place = (jnp.arange(2 * nh)[None, :] == ((r % nh) * 2 + r // nh)[:, None]).astype(F32)

    def page(g_, nd):
        return lambda b, s, pt: (layer, pt[b, s * G + g_]) + (0,) * nd

    tok = lambda rr, w: pl.BlockSpec((1, rr, w), lambda b, s, pt: (b, 0, 0))
    in_specs = ([pl.BlockSpec(memory_space=pltpu.SMEM), tok(8, DH), tok(8, 8 * DH), tok(8, DH), tok(8, 128),
                 pl.BlockSpec((1, 128), lambda b, s, pt: (0, 0))]
                + [pl.BlockSpec((None, None) + cache_kT.shape[2:], page(g_, 3)) for g_ in range(G)]
                + [pl.BlockSpec((None, None, vrows, 128), page(g_, 2)) for g_ in range(G)])
    return pl.pallas_call(
        functools.partial(_diff_decode_body, G=G, out_scale=out_scale),
        out_shape=jax.ShapeDtypeStruct((B, 4, 128), F32),
        grid_spec=pltpu.PrefetchScalarGridSpec(
            num_scalar_prefetch=1, grid=(B, n_pages // G), in_specs=in_specs,
            out_specs=tok(4, 128), scratch_shapes=_decode_scratch(128)),
        compiler_params=_cp(2),
        name="diff_decode",
    )(page_table, lam, q, _spread_rows(q, place), kn, vn, g, *([cache_kT] * G), *([cache_v] * G))


def _rows_time_major(ref, scr, tc, nb, offset=0):
    for t in range(tc):
        scr[offset + t * nb:offset + (t + 1) * nb, :] = ref[:, t, :]


def _s5_body(u_ref, h0r_ref, h0i_ref, ar_ref, ai_ref, bm_ref, cr_ref, ci_ref, d_ref, wg_ref,
             o_ref, hr_out, hi_out, u_scr, br_scr, bi_scr, hr_scr, hi_scr, *, tc, nb):
    i = pl.program_id(0)
    LB = S5_LANES // S5_BLOCKS
    UB = GROUP_WIDTH // S5_BLOCKS

    @pl.when(i == 0)
    def _():
        hr_scr[...] = h0r_ref[...]
        hi_scr[...] = h0i_ref[...]

    _rows_time_major(u_ref, u_scr, tc, nb)
    u = u_scr[...]
    ys = []
    for b in range(S5_BLOCKS):
        sl = slice(b * LB, (b + 1) * LB)
        bb = jnp.dot(u[:, b * UB:(b + 1) * UB].astype(BF), bm_ref[b], preferred_element_type=F32)
        br_scr[:, sl] = bb[:, :LB]
        bi_scr[:, sl] = bb[:, LB:]
        ar = jnp.broadcast_to(ar_ref[:, sl], (nb, LB))
        ai = jnp.broadcast_to(ai_ref[:, sl], (nb, LB))

        def step(t, h, sl=sl, ar=ar, ai=ai):
            hr, hi = h
            r0 = pl.multiple_of(t * nb, nb)
            nr = ar * hr - ai * hi + br_scr[pl.ds(r0, nb), sl]
            ni = ar * hi + ai * hr + bi_scr[pl.ds(r0, nb), sl]
            br_scr[pl.ds(r0, nb), sl] = nr
            bi_scr[pl.ds(r0, nb), sl] = ni
            return nr, ni

        hr, hi = lax.fori_loop(0, tc, step, (hr_scr[:, sl], hi_scr[:, sl]))
        hr_scr[:, sl] = hr
        hi_scr[:, sl] = hi
        ys.append(jnp.dot(br_scr[:, sl].astype(BF), cr_ref[b], preferred_element_type=F32)
                  + jnp.dot(bi_scr[:, sl].astype(BF), ci_ref[b], preferred_element_type=F32))
    y = jnp.concatenate(ys, axis=-1) + d_ref[...] * u
    g = jax.nn.gelu(y)
    out = g * jax.nn.sigmoid(jnp.dot(g.astype(BF), wg_ref[...], preferred_element_type=F32))
    for t in range(tc):
        o_ref[:, t, :] = out[t * nb:(t + 1) * nb, :]

    @pl.when(i == pl.num_programs(0) - 1)
    def _():
        hr_out[...] = hr_scr[...]
        hi_out[...] = hi_scr[...]


def _s5(u, h0r, h0i, p):
    nb, T, W = u.shape
    tc = min(T, 64)
    assert T % tc == 0
    rows = tc * nb
    st = jax.ShapeDtypeStruct((nb, S5_LANES), F32)
    seq_spec = pl.BlockSpec((nb, tc, W), lambda i: (0, i, 0))
    return pl.pallas_call(
        functools.partial(_s5_body, tc=tc, nb=nb),
        out_shape=[jax.ShapeDtypeStruct((nb, T, W), F32), st, st],
        grid=(T // tc,),
        in_specs=[seq_spec,
                  _full((nb, S5_LANES)), _full((nb, S5_LANES)),
                  _full((1, S5_LANES)), _full((1, S5_LANES)),
                  _full(p["bm"].shape), _full(p["cr"].shape), _full(p["ci"].shape),
                  _full((1, W)), _full(p["wglu"].shape)],
        out_specs=[seq_spec,
                   pl.BlockSpec((nb, S5_LANES), lambda i: (0, 0)),
                   pl.BlockSpec((nb, S5_LANES), lambda i: (0, 0))],
        scratch_shapes=[pltpu.VMEM((rows, W), F32),
                        pltpu.VMEM((rows, S5_LANES), F32), pltpu.VMEM((rows, S5_LANES), F32),
                        pltpu.VMEM((nb, S5_LANES), F32), pltpu.VMEM((nb, S5_LANES), F32)],
        compiler_params=_cp(1),
        name="s5",
    )(u, h0r, h0i, p["ar"], p["ai"], p["bm"], p["cr"], p["ci"], p["d"], p["wglu"])


def _lru_body(xb_ref, gate_ref, cv0_ref, h0_ref, cw_ref, cb_ref, wa_ref, ba_ref, wx_ref, bx_ref, nsp_ref,
              o_ref, cv_out, h_out, xp_scr, a_scr, b_scr, h_scr, *, tc, nb):
    i = pl.program_id(0)
    rows = tc * nb
    tail = (CONV_W - 1) * nb

    @pl.when(i == 0)
    def _():
        xp_scr[0:tail, :] = cv0_ref[...]
        h_scr[...] = h0_ref[...]

    @pl.when(i > 0)
    def _():
        xp_scr[0:tail, :] = xp_scr[rows:rows + tail, :]

    _rows_time_major(xb_ref, xp_scr, tc, nb, offset=tail)
    xc = cb_ref[...] + cw_ref[0:1, :] * xp_scr[0:rows, :]
    for j in range(1, CONV_W):
        xc = xc + cw_ref[j:j + 1, :] * xp_scr[j * nb:j * nb + rows, :]
    xcb = xc.astype(BF)
    r = jax.nn.sigmoid(jnp.dot(xcb, wa_ref[...], preferred_element_type=F32) + ba_ref[...])
    ig = jax.nn.sigmoid(jnp.dot(xcb, wx_ref[...], preferred_element_type=F32) + bx_ref[...])
    log_a = nsp_ref[...] * r
    a_scr[...] = jnp.exp(log_a)
    th = jnp.tanh(log_a)
    b_scr[...] = jnp.sqrt(-2.0 * th / (1.0 - th)) * (ig * xc)

    def step(t, h):
        r0 = pl.multiple_of(t * nb, nb)
        h = a_scr[pl.ds(r0, nb), :] * h + b_scr[pl.ds(r0, nb), :]
        b_scr[pl.ds(r0, nb), :] = h
        return h

    h = lax.fori_loop(0, tc, step, h_scr[...])
    h_scr[...] = h
    for t in range(tc):
        o_ref[:, t, :] = jax.nn.gelu(gate_ref[:, t, :]) * b_scr[t * nb:(t + 1) * nb, :]

    @pl.when(i == pl.num_programs(0) - 1)
    def _():
        cv_out[...] = xp_scr[rows:rows + tail, :]
        h_out[...] = h


def _lru(xb, gate, cv0, h0, p):
    nb, T, W = xb.shape
    tc = min(T, 128)
    assert T % tc == 0
    rows = tc * nb
    tail = (CONV_W - 1) * nb
    row_spec = pl.BlockSpec((nb, tc, W), lambda i: (0, i, 0))
    vec = _full((1, W))
    return pl.pallas_call(
        functools.partial(_lru_body, tc=tc, nb=nb),
        out_shape=[jax.ShapeDtypeStruct((nb, T, W), F32), jax.ShapeDtypeStruct((tail, W), F32),
                   jax.ShapeDtypeStruct((nb, W), F32)],
        grid=(T // tc,),
        in_specs=[row_spec, row_spec, _full((tail, W)), _full((nb, W)), _full((CONV_W, W)), vec,
                  _full((W, W)), vec, _full((W, W)), vec, vec],
        out_specs=[row_spec, pl.BlockSpec((tail, W), lambda i: (0, 0)), pl.BlockSpec((nb, W), lambda i: (0, 0))],
        scratch_shapes=[pltpu.VMEM((rows + tail, W), F32), pltpu.VMEM((rows, W), F32),
                        pltpu.VMEM((rows, W), F32), pltpu.VMEM((nb, W), F32)],
        compiler_params=_cp(1),
        name="rglru",
    )(xb, gate, cv0, h0, p["cw"], p["cb"], p["wa"], p["ba"], p["wx"], p["bx"], p["nsp"])


def _prep_ffn(wg, wu, wd):
    d, f = wg.shape
    nch = f // FF_CHUNK
    cols = lambda w: w.astype(BF).reshape(d, nch, FF_CHUNK).transpose(1, 0, 2)
    return cols(wg), cols(wu), wd.astype(BF).reshape(nch, FF_CHUNK, d)


def _prep_s5(a_re, a_im, b_re, b_im, c_re, c_im, d_skip, log_dt, w_glu):
    dt = jnp.exp(log_dt)[:, None]
    mag = jnp.exp(a_re * dt)
    ab_re, ab_im = mag * jnp.cos(a_im * dt), mag * jnp.sin(a_im * dt)
    den = a_re * a_re + a_im * a_im
    nr, ni = ab_re - 1.0, ab_im
    co_re, co_im = (nr * a_re + ni * a_im) / den, (ni * a_re - nr * a_im) / den
    bt_re = co_re[..., None] * b_re - co_im[..., None] * b_im
    bt_im = co_re[..., None] * b_im + co_im[..., None] * b_re
    gpb = S5_GROUPS // S5_BLOCKS
    eye = jnp.eye(gpb, dtype=F32)

    def blk_in(bt):
        x = bt.reshape(S5_BLOCKS, gpb, S5_STATE, S5_CH)
        return jnp.einsum('bgph,gk->bghkp', x, eye).reshape(S5_BLOCKS, gpb * S5_CH, gpb * S5_STATE)

    def blk_out(c):
        x = c.reshape(S5_BLOCKS, gpb, S5_CH, S5_STATE)
        return jnp.einsum('bghp,gk->bgpkh', x, eye).reshape(S5_BLOCKS, gpb * S5_STATE, gpb * S5_CH)

    return {"ar": ab_re.reshape(1, S5_LANES), "ai": ab_im.reshape(1, S5_LANES),
            "bm": jnp.concatenate([blk_in(bt_re), blk_in(bt_im)], axis=-1).astype(BF),
            "cr": blk_out(c_re).astype(BF), "ci": blk_out(-c_im).astype(BF),
            "d": d_skip.reshape(1, GROUP_WIDTH), "wglu": w_glu.astype(BF)}


def _prep_lru(conv_w, conv_b, w_a, b_a, w_x, b_x, lam):
    eye = jnp.eye(LRU_BLOCKS, dtype=F32)
    W = GROUP_WIDTH
    dense = lambda w: jnp.einsum('nde,nm->ndme', w, eye).reshape(W, W).astype(BF)
    row = lambda v: v.reshape(1, W)
    return {"cw": conv_w, "cb": row(conv_b), "wa": dense(w_a), "ba": row(b_a), "wx": dense(w_x),
            "bx": row(b_x), "nsp": row(-LRU_C * jax.nn.softplus(-lam))}


def _heads(a, B, T, width, transposed):
    nh = GROUP_WIDTH // width
    if transposed:
        return a.reshape(B, nh, width, T).transpose(0, 3, 1, 2)
    return a.reshape(B, T, nh, width)


def _trunk(x, B, T, P, st):
    prompt = st is None
    W = GROUP_WIDTH
    kd, vd, sr, si, kf, vf, lf, cv, lr = ([] for _ in range(9))
    depth = len(P["ffn"])
    for l in range(depth):
        g = P["norm_g"][l]
        x = _ffn(x, g[0], P["ffn"][l][0])
        if l % 2 == 0:
            e = l // 2
            lam_init = 0.8 - 0.6 * math.exp(-0.3 * l)
            lam = P["lam"][e]
            res = _inproj_even(x, g[1], P["w_in_even"][e], prompt, T)
            u, q, k, v = res[:4]
            if prompt:
                kb, vTb = res[4:]
                attn = _attn_prompt(q, kb.reshape(B, T, W), vTb, lam, P["sel"], P["subln_g"][e],
                                    fox=False, out_scale=1.0 - lam_init).reshape(B * T, W)
                h0r = h0i = jnp.zeros((B, S5_LANES), F32)
            else:
                nh = W // (2 * DH)
                mh = lambda a: a.reshape(B, nh, 2, DH).transpose(0, 2, 1, 3).reshape(B, 2 * nh, DH)
                vn = v.reshape(B, nh, 2 * DH)
                attn = _diff_decode(st["page_table"], e, lam, mh(q), mh(k),
                                    jnp.concatenate([vn, vn], axis=1), P["subln_g"][e],
                                    st["k_diff"], st["v_diff"], 1.0 - lam_init).reshape(B, W)
                h0r = st["s5_re"][e].reshape(B, S5_LANES)
                h0i = st["s5_im"][e].reshape(B, S5_LANES)
            rec, hr, hi = _s5(u.reshape(B, T, W), h0r, h0i, P["s5"][e])
            rec = rec.reshape(B * T, W)
            x = _outproj(x, rec, attn, *P["w_out_even"][e])
            kd.append(_heads(k, B, T, DH, transposed=prompt))
            vd.append(_heads(v, B, T, 2 * DH, transposed=False))
            sr.append(hr.reshape(B, S5_GROUPS, S5_STATE))
            si.append(hi.reshape(B, S5_GROUPS, S5_STATE))
        else:
            o = l // 2
            H = N_FOX_HEADS
            res = _inproj_odd(x, g[1], P["w_in_odd"][o], P["w_f"][o], P["b_f"][o], prompt, T)
            xb, gate, q, k, v, logf = res[:6]
            if prompt:
                kb, vTb = res[6:8]
                pieces = jnp.stack(res[8:11], axis=-1)
                ck = jnp.pad(pieces.reshape(B, T, H // 2, 6), ((0, 0), (0, 0), (0, 0), (0, 10)))
                k_aug = jnp.concatenate([kb.reshape(B, T, H // 2, 2 * DH), ck], axis=-1).transpose(0, 2, 1, 3)
                attn = _attn_prompt(q, k_aug, vTb, P["lam"][0], P["sel"], P["subln_g"][0],
                                    fox=True, out_scale=1.0).reshape(B * T, W)
                cv0 = jnp.zeros(((CONV_W - 1) * B, W), F32)
                h0 = jnp.zeros((B, W), F32)
            else:
                hd = lambda a: a.reshape(B, H, DH)
                attn = _fox_decode(st["page_table"], o, hd(q), hd(k), hd(v), logf.reshape(B, H, 1),
                                   st["k_fox"], st["v_fox"], st["logfT"]).reshape(B, W)
                cv0 = st["conv"][o].transpose(1, 0, 2).reshape((CONV_W - 1) * B, W)
                h0 = st["lru"][o]
            rec, cv_new, h_new = _lru(xb.reshape(B, T, W), gate.reshape(B, T, W), cv0, h0, P["lru"][o])
            rec = rec.reshape(B * T, W)
            x = _outproj(x, rec, attn, *P["w_out_odd"][o])
            kf.append(_heads(k, B, T, DH, transposed=prompt))
            vf.append(_heads(v, B, T, DH, transposed=prompt))
            lf.append(logf.reshape(B, T, H))
            cv.append(cv_new.reshape(CONV_W - 1, B, W).transpose(1, 0, 2))
            lr.append(h_new)
        x = _ffn(x, g[2], P["ffn"][l][1], final_g=P["final_norm_g"] if l == depth - 1 else None)
    return (x.reshape(B, T, D_MODEL), jnp.stack(kd), jnp.stack(vd), jnp.stack(sr), jnp.stack(si),
            jnp.stack(kf), jnp.stack(vf), jnp.stack(lf), jnp.stack(cv), jnp.stack(lr))


def kernel(x_prompt, x_sample, cache_k_diff, cache_v_diff, state_s5_re, state_s5_im, cache_k_fox, cache_v_fox, cache_logf_fox, state_conv, state_lru, page_table, norm_g, final_norm_g, w_ffn_gate, w_ffn_up, w_ffn_down, w_in_even, w_out_even, s5_a_re, s5_a_im, s5_b_re, s5_b_im, s5_c_re, s5_c_im, s5_d, s5_log_dt, s5_w_glu, lambda_q1, lambda_k1, lambda_q2, lambda_k2, diff_subln_g, w_in_odd, w_out_odd, fox_b_f, conv_w, conv_b, lru_w_a, lru_b_a, lru_w_x, lru_b_x, lru_lambda):
    depth = w_ffn_gate.shape[0]
    n_even, n_odd = w_in_even.shape[0], w_in_odd.shape[0]
    W = GROUP_WIDTH
    H = N_FOX_HEADS
    main = 2 * W + 3 * W
    lam = (jnp.exp(jnp.sum(lambda_q1 * lambda_k1, axis=-1)) - jnp.exp(jnp.sum(lambda_q2 * lambda_k2, axis=-1)))
    lam_init = jnp.array([0.8 - 0.6 * math.exp(-0.3 * 2 * e) for e in range(n_even)], F32)
    P = {
        "norm_g": norm_g, "final_norm_g": final_norm_g,
        "ffn": [[_prep_ffn(w_ffn_gate[l, i], w_ffn_up[l, i], w_ffn_down[l, i]) for i in range(2)]
                for l in range(depth)],
        "w_in_even": w_in_even.astype(BF),
        "w_out_even": [(w_out_even[e, :W].astype(BF), w_out_even[e, W:].astype(BF)) for e in range(n_even)],
        "s5": [_prep_s5(s5_a_re[e], s5_a_im[e], s5_b_re[e], s5_b_im[e], s5_c_re[e], s5_c_im[e],
                        s5_d[e], s5_log_dt[e], s5_w_glu[e]) for e in range(n_even)],
        "lam": (lam + lam_init).reshape(n_even, 1),
        "subln_g": diff_subln_g.reshape(n_even, 1, 2 * DH),
        "w_in_odd": w_in_odd[:, :, :main].astype(BF),
        "w_f": jnp.pad(w_in_odd[:, :, main:], ((0, 0), (0, 0), (0, 128 - H))).astype(BF),
        "b_f": jnp.pad(fox_b_f, ((0, 0), (0, 128 - H))).reshape(n_odd, 1, 128),
        "w_out_odd": [(w_out_odd[o, :W].astype(BF), w_out_odd[o, W:].astype(BF)) for o in range(n_odd)],
        "lru": [_prep_lru(conv_w[o], conv_b[o], lru_w_a[o], lru_b_a[o], lru_w_x[o], lru_b_x[o], lru_lambda[o])
                for o in range(n_odd)],
        "sel": (lax.broadcasted_iota(jnp.int32, (16, 2 * ATTN_QB), 0) // 3
                == lax.broadcasted_iota(jnp.int32, (16, 2 * ATTN_QB), 1) // ATTN_QB).astype(BF),
    }
    Bp, Tp, _ = x_prompt.shape
    Bs, Ts, _ = x_sample.shape
    n_phys, page = cache_k_diff.shape[1], cache_k_diff.shape[2]
    st = {
        "page_table": page_table,
        "k_diff": cache_k_diff.transpose(0, 1, 3, 4, 2),
        "v_diff": cache_v_diff.reshape(n_even, n_phys, page * (W // (2 * DH)), 2 * DH),
        "k_fox": cache_k_fox.transpose(0, 1, 3, 4, 2),
        "v_fox": cache_v_fox.transpose(0, 1, 3, 4, 2),
        "logfT": cache_logf_fox.transpose(0, 1, 3, 2),
        "s5_re": state_s5_re, "s5_im": state_s5_im, "conv": state_conv, "lru": state_lru,
    }
    outs_p = _trunk(x_prompt.reshape(Bp * Tp, D_MODEL), Bp, Tp, P, None)
    outs_s = _trunk(x_sample.reshape(Bs * Ts, D_MODEL), Bs, Ts, P, st)
    return (outs_p[0], outs_s[0]) + tuple(outs_p[1:]) + tuple(outs_s[1:])
```

```python
import functools
import math

import jax
import jax.numpy as jnp
from jax import lax
from jax.experimental import pallas as pl
from jax.experimental.pallas import tpu as pltpu

F32 = jnp.float32
BF = jnp.bfloat16

D_MODEL = 1024
GROUP_WIDTH = 512
DH = 64
S5_GROUPS = 32
S5_STATE = 64
S5_CH = 16
S5_LANES = S5_GROUPS * S5_STATE
S5_BLOCKS = 4
LRU_BLOCKS = 8
LRU_C = 8.0
CONV_W = 4
N_FOX_HEADS = 8
FF_CHUNK = 256
EPS = 1e-6
NEG = -1e30
ATTN_SCALE = DH ** -0.5
LOG2E = math.log2(math.e)
PROMPT_Q_SCALE = ATTN_SCALE * LOG2E

VMEM_LIMIT_V7X = 56 * 1024 * 1024
PAGES_PER_STEP = 8


def _cp(n_axes, vmem=VMEM_LIMIT_V7X):
    return pltpu.CompilerParams(dimension_semantics=("arbitrary",) * n_axes,
                                vmem_limit_bytes=vmem)


def _rms(x, g):
    return x * lax.rsqrt(jnp.mean(x * x, axis=-1, keepdims=True) + EPS) * g


def _full(shape, single=True):
    nd = len(shape)
    kw = {"pipeline_mode": pl.Buffered(1)} if single else {}
    return pl.BlockSpec(shape, lambda *_: (0,) * nd, **kw)


def _row_tile(n, want):
    t = min(n, want)
    assert n % t == 0
    return t


def _ffn_body(*refs, n_chunks, final, mixed):
    if mixed:
        x_ref, ma_ref, mb_ref, wa_ref, wb_ref, g_ref, wg_ref, wu_ref, wd_ref, fg_ref, o_ref, hn_ref = refs
        x = (x_ref[...]
             + jnp.dot(ma_ref[...].astype(BF), wa_ref[...], preferred_element_type=F32)
             + jnp.dot(mb_ref[...].astype(BF), wb_ref[...], preferred_element_type=F32))
    else:
        x_ref, g_ref, wg_ref, wu_ref, wd_ref, fg_ref, o_ref, hn_ref = refs
        x = x_ref[...]
    hn_ref[...] = _rms(x, g_ref[...]).astype(BF)
    o_ref[...] = x

    def chunk(j, c):
        hn = hn_ref[...]
        cols = pl.ds(pl.multiple_of(j * FF_CHUNK, FF_CHUNK), FF_CHUNK)
        a = jnp.dot(hn, wg_ref[:, cols], preferred_element_type=F32)
        b = jnp.dot(hn, wu_ref[:, cols], preferred_element_type=F32)
        h = (a * jax.nn.sigmoid(a) * b).astype(BF)
        o_ref[...] += 0.5 * jnp.dot(h, wd_ref[j], preferred_element_type=F32)
        return c

    lax.fori_loop(0, n_chunks, chunk, 0)
    if final:
        o_ref[...] = _rms(o_ref[...], fg_ref[...])


def _ffn(x, g, w, final_g=None, mix=None):
    n = x.shape[0]
    wg, wu, wd = w
    nch = wd.shape[0]
    tm = _row_tile(n, 1024)
    fg = g if final_g is None else final_g
    rows = lambda width: pl.BlockSpec((tm, width), lambda i: (i, 0))
    mix_specs, mix_args = [], []
    if mix is not None:
        mix_specs = [rows(GROUP_WIDTH), rows(GROUP_WIDTH), _full(mix[2].shape), _full(mix[3].shape)]
        mix_args = list(mix)
    return pl.pallas_call(
        functools.partial(_ffn_body, n_chunks=nch, final=final_g is not None, mixed=mix is not None),
        out_shape=jax.ShapeDtypeStruct((n, D_MODEL), F32),
        grid=(n // tm,),
        in_specs=[rows(D_MODEL)] + mix_specs
        + [_full((1, D_MODEL)), _full(wg.shape), _full(wu.shape), _full(wd.shape), _full((1, D_MODEL))],
        out_specs=rows(D_MODEL),
        scratch_shapes=[pltpu.VMEM((tm, D_MODEL), BF)],
        compiler_params=_cp(1),
        name="ffn_mix" if mix is not None else "ffn",
    )(x, *mix_args, g.reshape(1, D_MODEL), wg, wu, wd, fg.reshape(1, D_MODEL))


def _store(ref, val):
    if len(ref.shape) == 3:
        ref[0] = val.T.astype(ref.dtype)
    else:
        ref[...] = val.astype(ref.dtype)


def _proj_out(n, tm, seq_len, dtype, transposed):
    W = GROUP_WIDTH
    if not transposed:
        return jax.ShapeDtypeStruct((n, W), dtype), pl.BlockSpec((tm, W), lambda i: (i, 0))
    tps = seq_len // tm
    return (jax.ShapeDtypeStruct((n // seq_len, W, seq_len), dtype),
            pl.BlockSpec((1, W, tm), lambda i: (i // tps, 0, i % tps)))


def _inproj_even_body(x_ref, g_ref, w_ref, u_ref, q_ref, k_ref, v_ref, *bf_refs, q_scale):
    hn = _rms(x_ref[...], g_ref[...]).astype(BF)
    W = GROUP_WIDTH

    def col(c):
        return jnp.dot(hn, w_ref[:, c * W:(c + 1) * W], preferred_element_type=F32)

    u_ref[...] = col(0)
    _store(q_ref, col(1) * q_scale)
    k = col(2)
    v = col(3)
    _store(k_ref, k)
    _store(v_ref, v)
    if bf_refs:
        _store(bf_refs[0], k)
        _store(bf_refs[1], v)


def _inproj_even(x, g, w, prompt, seq_len):
    n = x.shape[0]
    tm = _row_tile(n, 512)
    out = functools.partial(_proj_out, n, tm, seq_len)
    outs = [out(F32, False), out(BF if prompt else F32, prompt), out(F32, prompt), out(F32, False)]
    if prompt:
        assert seq_len % tm == 0
        outs += [out(BF, False), out(BF, True)]
    return pl.pallas_call(
        functools.partial(_inproj_even_body, q_scale=PROMPT_Q_SCALE if prompt else ATTN_SCALE),
        out_shape=[o[0] for o in outs],
        grid=(n // tm,),
        in_specs=[pl.BlockSpec((tm, D_MODEL), lambda i: (i, 0)), _full((1, D_MODEL)), _full(w.shape)],
        out_specs=[o[1] for o in outs],
        compiler_params=_cp(1),
        name="inproj_even",
    )(x, g.reshape(1, D_MODEL), w)


def _split3(x):
    h = x.astype(BF)
    r = x - h.astype(F32)
    m = r.astype(BF)
    lo = (r - m.astype(F32)).astype(BF)
    return h, m, lo


def _dot3(pieces, w, dims=None):
    h, m, lo = pieces
    if dims is None:
        d = lambda a: jnp.dot(a, w, preferred_element_type=F32)
    else:
        d = lambda a: lax.dot_general(a, w, dims, preferred_element_type=F32)
    return (d(lo) + d(m)) + d(h)


def _log_sigmoid(z):
    return -(jnp.maximum(-z, 0.0) + jnp.log1p(jnp.exp(-jnp.abs(z))))


CUM_BLOCK = 256


def _inproj_odd_body(x_ref, g_ref, w_ref, wf_ref, bf_ref, xb_ref, gate_ref, q_ref, k_ref, v_ref,
                     lf_ref, *rest, tiles_per_seq, q_scale):
    hn = _rms(x_ref[...], g_ref[...]).astype(BF)
    W = GROUP_WIDTH

    def col(c):
        return jnp.dot(hn, w_ref[:, c * W:(c + 1) * W], preferred_element_type=F32)

    xb_ref[...] = col(0)
    gate_ref[...] = col(1)
    _store(q_ref, col(2) * q_scale)
    k = col(3)
    v = col(4)
    _store(k_ref, k)
    _store(v_ref, v)
    f = jnp.dot(hn, wf_ref[...], preferred_element_type=F32) + bf_ref[...]
    lf = _log_sigmoid(f)
    lf_ref[...] = lf[:, :N_FOX_HEADS]
    if rest:
        kb_ref, vb_ref, ch_ref, cm_ref, cl_ref, carry_ref = rest
        _store(kb_ref, k)
        _store(vb_ref, v)

        @pl.when(pl.program_id(0) % tiles_per_seq == 0)
        def _():
            carry_ref[...] = jnp.zeros_like(carry_ref)

        tm = lf.shape[0]
        nb = tm // CUM_BLOCK
        r = lax.broadcasted_iota(jnp.int32, (CUM_BLOCK, CUM_BLOCK), 0)
        c = lax.broadcasted_iota(jnp.int32, (CUM_BLOCK, CUM_BLOCK), 1)
        tril = jnp.where(c <= r, 1.0, 0.0).astype(BF)
        carry = carry_ref[0:1, :]
        for b in range(nb):
            blk = lf[b * CUM_BLOCK:(b + 1) * CUM_BLOCK, :]
            h, m, lo = _split3(blk)
            d = lambda a: jnp.dot(tril, a, preferred_element_type=F32)
            cum = ((d(lo) + d(m)) + d(h)) + carry
            rows = slice(b * CUM_BLOCK, (b + 1) * CUM_BLOCK)
            for piece, ref in zip(_split3(cum * (-LOG2E)), (ch_ref, cm_ref, cl_ref)):
                ref[rows, :] = piece[:, :N_FOX_HEADS]
            carry = cum[CUM_BLOCK - 1:CUM_BLOCK, :]
        carry_ref[0:1, :] = carry


def _inproj_odd(x, g, w, wf, bf, prompt, seq_len):
    n = x.shape[0]
    tm = _row_tile(n, 512)
    H = N_FOX_HEADS
    out = functools.partial(_proj_out, n, tm, seq_len)
    per_head = lambda dt: (jax.ShapeDtypeStruct((n, H), dt), pl.BlockSpec((tm, H), lambda i: (i, 0)))
    outs = [out(F32, False), out(F32, False), out(BF if prompt else F32, prompt), out(F32, prompt),
            out(F32, prompt), per_head(F32)]
    scratch = []
    tiles_per_seq = 1
    if prompt:
        assert seq_len % tm == 0 and tm % CUM_BLOCK == 0
        tiles_per_seq = seq_len // tm
        outs += [out(BF, False), out(BF, True)] + [per_head(BF)] * 3
        scratch = [pltpu.VMEM((8, 128), F32)]
    return pl.pallas_call(
        functools.partial(_inproj_odd_body, tiles_per_seq=tiles_per_seq,
                          q_scale=PROMPT_Q_SCALE if prompt else ATTN_SCALE),
        out_shape=[o[0] for o in outs],
        grid=(n // tm,),
        in_specs=[pl.BlockSpec((tm, D_MODEL), lambda i: (i, 0)), _full((1, D_MODEL)),
                  _full(w.shape), _full(wf.shape), _full((1, 128))],
        out_specs=[o[1] for o in outs],
        scratch_shapes=scratch,
        compiler_params=_cp(1),
        name="inproj_odd",
    )(x, g.reshape(1, D_MODEL), w, wf, bf)


ATTN_QB = 256
ATTN_TK = 512
ATTN_SUB = 64


def _attn_body(lam_ref, qT_ref, k_ref, vT_ref, sel_ref, g_ref, o_ref, q2_scr, s_scr, mx_scr, p_scr, acc_scr,
               *, fox, seq_len, out_scale):
    n2 = 2 * ATTN_QB
    nblk = seq_len // ATTN_QB
    rowd = lax.broadcasted_iota(jnp.int32, (128, ATTN_QB), 0)
    lo = rowd < DH
    koq = (lax.broadcasted_iota(jnp.int32, (ATTN_SUB, n2), 0)
           - (lax.broadcasted_iota(jnp.int32, (ATTN_SUB, n2), 1) & (ATTN_QB - 1)))
    nsub = ATTN_TK // ATTN_SUB

    def load_queries(h):
        qT = qT_ref[0, :, pl.ds(pl.multiple_of(h * ATTN_QB, ATTN_QB), ATTN_QB)]
        zero = jnp.zeros_like(qT)
        q2 = jnp.concatenate([jnp.where(lo, qT, zero), jnp.where(lo, zero, qT)], axis=1)
        if fox:
            q2 = jnp.concatenate([q2, sel_ref[...]], axis=0)
        q2_scr[h & 1] = q2

    def score_stage(h, c, buf):
        q2 = q2_scr[h & 1]
        q0 = h * ATTN_QB
        start = pl.multiple_of(c * ATTN_TK, ATTN_TK)
        mx = jnp.full((ATTN_SUB, n2), NEG, F32)
        for sb in range(nsub):
            r0 = start + sb * ATTN_SUB
            krows = k_ref[0, 0, pl.ds(r0, ATTN_SUB), :] if fox else k_ref[0, pl.ds(r0, ATTN_SUB), :]
            s = jnp.dot(krows, q2, preferred_element_type=F32)
            s = jnp.where(koq <= q0 - r0, s, NEG)
            s_scr[buf, sb * ATTN_SUB:(sb + 1) * ATTN_SUB, :] = s
            mx = jnp.maximum(mx, s)
        mx_scr[buf] = mx

    def softmax_stage(c, buf, carry):
        m, l = carry
        mn = jnp.maximum(m, jnp.max(mx_scr[buf], axis=0, keepdims=True))
        al = jnp.exp2(m - mn)
        psum = jnp.zeros((ATTN_SUB, n2), F32)
        for sb in range(nsub):
            p = jnp.exp2(s_scr[buf, sb * ATTN_SUB:(sb + 1) * ATTN_SUB, :] - mn)
            psum = psum + p
            p_scr[sb * ATTN_SUB:(sb + 1) * ATTN_SUB, :] = p.astype(BF)
        l = al * l + jnp.sum(psum, axis=0, keepdims=True)
        start = pl.multiple_of(c * ATTN_TK, ATTN_TK)
        vT = vT_ref[0, :, pl.ds(start, ATTN_TK)]
        acc_scr[...] = al * acc_scr[...] + jnp.dot(vT, p_scr[...], preferred_element_type=F32)
        return mn, l

    def query_block(h, par):
        n_last = (h * ATTN_QB) // ATTN_TK
        acc_scr[...] = jnp.zeros_like(acc_scr)

        def step(c, carry):
            carry = softmax_stage(c, (par + c) & 1, carry)
            score_stage(h, c + 1, (par + c + 1) & 1)
            return carry

        carry = (jnp.full((1, n2), NEG, F32), jnp.zeros((1, n2), F32))
        carry = lax.fori_loop(0, n_last, step, carry)
        nxt = jnp.minimum(h + 1, nblk - 1)
        load_queries(nxt)
        m, l = softmax_stage(n_last, (par + n_last) & 1, carry)
        score_stage(nxt, 0, (par + n_last + 1) & 1)
        oT = acc_scr[...] / l
        oa, ob = oT[:, :ATTN_QB], oT[:, ATTN_QB:]
        if fox:
            o = jnp.where(lo, oa, ob).T
        else:
            o = _rms((oa - lam_ref[0] * ob).T, g_ref[...]) * out_scale
        o_ref[0, pl.ds(pl.multiple_of(h * ATTN_QB, ATTN_QB), ATTN_QB), :] = o
        return (par + n_last + 1) & 1

    load_queries(0)
    score_stage(0, 0, 0)
    lax.fori_loop(0, nblk, query_block, 0)


def _attn_prompt(qT, k, vT, lam, sel, g, *, fox, out_scale):
    B, _, T = qT.shape
    assert T % ATTN_TK == 0
    npair = GROUP_WIDTH // 128
    n2 = 2 * ATTN_QB
    kdim = k.shape[-1] if fox else 128
    return pl.pallas_call(
        functools.partial(_attn_body, fox=fox, seq_len=T, out_scale=out_scale),
        out_shape=jax.ShapeDtypeStruct((B, T, GROUP_WIDTH), F32),
        grid=(B, npair),
        in_specs=[pl.BlockSpec(memory_space=pltpu.SMEM),
                  pl.BlockSpec((1, 128, T), lambda b, j: (b, j, 0)),
                  pl.BlockSpec((1, 1, T, kdim), lambda b, j: (b, j, 0, 0)) if fox
                  else pl.BlockSpec((1, T, 128), lambda b, j: (b, 0, j)),
                  pl.BlockSpec((1, 128, T), lambda b, j: (b, j, 0)),
                  pl.BlockSpec((16, n2), lambda b, j: (0, 0)),
                  pl.BlockSpec((1, 128), lambda b, j: (0, 0))],
        out_specs=pl.BlockSpec((1, T, 128), lambda b, j: (b, 0, j)),
        scratch_shapes=[pltpu.VMEM((2, kdim, n2), BF),
                        pltpu.VMEM((2, ATTN_TK, n2), F32), pltpu.VMEM((2, ATTN_SUB, n2), F32),
                        pltpu.VMEM((ATTN_TK, n2), BF), pltpu.VMEM((128, n2), F32)],
        compiler_params=_cp(2),
        name="attn_fox" if fox else "attn_diff",
    )(lam, qT, k, vT, sel, g)


def _online_update(scs, m, l):
    mx = functools.reduce(jnp.maximum, scs)
    mn = jnp.maximum(m, jnp.max(mx, axis=-1, keepdims=True))
    al = jnp.exp(m - mn)
    ps = [jnp.exp(sc - mn) for sc in scs]
    l = al * l + jnp.sum(functools.reduce(jnp.add, ps), axis=-1, keepdims=True)
    return mn, l, al, ps


_NT = (((1,), (1,)), ((), ()))


def _page2d(ref):
    h, d, r = ref.shape
    return ref[...].reshape(h * d, r)


def _fox_decode_body(pt_ref, q_ref, qbd_ref, kn_ref, vbd_ref, lfn_ref, tri_ref, *rest, G):
    k_refs, v_refs, lf_refs = rest[0:G], rest[G:2 * G], rest[2 * G:3 * G]
    o_ref = rest[3 * G]
    m_scr, l_scr, acc_scr, car_scr = rest[3 * G + 1:]
    s = pl.program_id(1)

    @pl.when(s == 0)
    def _():
        m_scr[...] = jnp.sum(q_ref[0] * kn_ref[0], axis=-1, keepdims=True)
        l_scr[...] = jnp.ones_like(l_scr)
        acc_scr[...] = vbd_ref[0]
        car_scr[...] = lfn_ref[0]

    H = N_FOX_HEADS
    qbd = qbd_ref[0].astype(BF)
    qk = [jnp.dot(qbd, _page2d(k_refs[g]).astype(BF), preferred_element_type=F32)
          for g in range(G)]
    lfT = jnp.concatenate([lf_refs[g][...] for g in range(G)], axis=0)
    bias = _dot3(_split3(lfT), tri_ref[...])
    car = car_scr[...]
    scs = []
    for g in range(G):
        scs.append(qk[g] + (bias[g * H:(g + 1) * H] + car))
        car = car + jnp.sum(lf_refs[g][...], axis=-1, keepdims=True)
    m, l, al, ps = _online_update(scs, m_scr[...], l_scr[...])
    pv = [lax.dot_general(ps[g].astype(BF), _page2d(v_refs[g]).astype(BF), _NT, preferred_element_type=F32)
          for g in range(G)]
    acc = al * acc_scr[...] + functools.reduce(jnp.add, pv)
    m_scr[...], l_scr[...], acc_scr[...], car_scr[...] = m, l, acc, car

    @pl.when(s == pl.num_programs(1) - 1)
    def _():
        own = (lax.broadcasted_iota(jnp.int32, acc.shape, 1) // DH
               == lax.broadcasted_iota(jnp.int32, acc.shape, 0))
        a = jnp.where(own, acc, 0.0)
        o_ref[0] = functools.reduce(jnp.add, [a[:, h * DH:(h + 1) * DH] for h in range(H)]) / l


def _diff_decode_body(pt_ref, lam_ref, q_ref, qbd_ref, kn_ref, vn_ref, g_ref, *rest, G, out_scale):
    k_refs, v_refs = rest[0:G], rest[G:2 * G]
    o_ref = rest[2 * G]
    m_scr, l_scr, acc_scr = rest[2 * G + 1:]
    s = pl.program_id(1)
    nh = 4

    @pl.when(s == 0)
    def _():
        m_scr[...] = jnp.sum(q_ref[0] * kn_ref[0], axis=-1, keepdims=True)
        l_scr[...] = jnp.ones_like(l_scr)
        acc_scr[...] = vn_ref[0]

    qbd = qbd_ref[0].astype(BF)
    scs = [jnp.dot(qbd, _page2d(k_refs[g]).astype(BF), preferred_element_type=F32)
           for g in range(G)]
    m, l, al, ps = _online_update(scs, m_scr[...], l_scr[...])
    page_rows = v_refs[0].shape[0] // nh
    head_of_row = lax.broadcasted_iota(jnp.int32, acc_scr.shape, 0) & (nh - 1)
    upd = jnp.zeros(acc_scr.shape, F32)
    for g in range(G):
        pb = ps[g].astype(BF)
        for h in range(nh):
            vh = v_refs[g][pl.ds(h, page_rows, stride=nh), :].astype(BF)
            upd = upd + jnp.where(head_of_row == h, jnp.dot(pb, vh, preferred_element_type=F32), 0.0)
    acc = al * acc_scr[...] + upd
    m_scr[...], l_scr[...], acc_scr[...] = m, l, acc

    @pl.when(s == pl.num_programs(1) - 1)
    def _():
        o = acc / l
        o = o[0:nh, :] - lam_ref[0] * o[nh:2 * nh, :]
        o_ref[0] = _rms(o, g_ref[...]) * out_scale


def _decode_scratch(width):
    return [pltpu.VMEM((8, 1), F32), pltpu.VMEM((8, 1), F32), pltpu.VMEM((8, width), F32)]


def _spread_rows(a, place):
    B, R, D = a.shape
    return jnp.einsum('brd,rj->brjd', a, place).reshape(B, R, place.shape[1] * D)


def _fox_decode(page_table, layer, q, kn, vn, lfn, cache_kT, cache_vT, cache_lfT):
    B, n_pages = page_table.shape
    G = math.gcd(PAGES_PER_STEP, n_pages)
    H = N_FOX_HEADS
    page_rows = cache_kT.shape[-1]
    eye = jnp.eye(H, dtype=F32)
    tri = (lax.broadcasted_iota(jnp.int32, (page_rows, page_rows), 0)
           > lax.broadcasted_iota(jnp.int32, (page_rows, page_rows), 1)).astype(BF)

    def page(g, nd):
        return lambda b, s, pt: (layer, pt[b, n_pages - 1 - (s * G + g)]) + (0,) * nd

    tok = lambda w: pl.BlockSpec((1, 8, w), lambda b, s, pt: (b, 0, 0))
    kv_block = (None, None, H, DH, page_rows)
    in_specs = ([tok(DH), tok(H * DH), tok(DH), tok(H * DH), tok(1),
                 pl.BlockSpec(tri.shape, lambda b, s, pt: (0, 0))]
                + [pl.BlockSpec(kv_block, page(g, 3)) for g in range(G)]
                + [pl.BlockSpec(kv_block, page(g, 3)) for g in range(G)]
                + [pl.BlockSpec((None, None, H, page_rows), page(g, 2)) for g in range(G)])
    return pl.pallas_call(
        functools.partial(_fox_decode_body, G=G),
        out_shape=jax.ShapeDtypeStruct((B, 8, DH), F32),
        grid_spec=pltpu.PrefetchScalarGridSpec(
            num_scalar_prefetch=1, grid=(B, n_pages // G), in_specs=in_specs,
            out_specs=tok(DH), scratch_shapes=_decode_scratch(H * DH) + [pltpu.VMEM((8, 1), F32)]),
        compiler_params=_cp(2),
        name="fox_decode",
    )(page_table, q, _spread_rows(q, eye), kn, _spread_rows(vn, eye), lfn, tri,
      *([cache_kT] * G), *([cache_vT] * G), *([cache_lfT] * G))


def _diff_decode(page_table, layer, lam, q, kn, vn, g, cache_kT, cache_v, out_scale):
    B, n_pages = page_table.shape
    G = math.gcd(PAGES_PER_STEP, n_pages)
    vrows = cache_v.shape[2]
    nh = 4
    r = jnp.arange(2 * nh)
    place = (jnp.arange(2 * nh)[None, :] == ((r % nh) * 2 + r // nh)[:, None]).astype(F32)

    def page(g_, nd):
        return lambda b, s, pt: (layer, pt[b, s * G + g_]) + (0,) * nd

    tok = lambda rr, w: pl.BlockSpec((1, rr, w), lambda b, s, pt: (b, 0, 0))
    in_specs = ([pl.BlockSpec(memory_space=pltpu.SMEM), tok(8, DH), tok(8, 8 * DH), tok(8, DH), tok(8, 128),
                 pl.BlockSpec((1, 128), lambda b, s, pt: (0, 0))]
                + [pl.BlockSpec((None, None) + cache_kT.shape[2:], page(g_, 3)) for g_ in range(G)]
                + [pl.BlockSpec((None, None, vrows, 128), page(g_, 2)) for g_ in range(G)])
    return pl.pallas_call(
        functools.partial(_diff_decode_body, G=G, out_scale=out_scale),
        out_shape=jax.ShapeDtypeStruct((B, 4, 128), F32),
        grid_spec=pltpu.PrefetchScalarGridSpec(
            num_scalar_prefetch=1, grid=(B, n_pages // G), in_specs=in_specs,
            out_specs=tok(4, 128), scratch_shapes=_decode_scratch(128)),
        compiler_params=_cp(2),
        name="diff_decode",
    )(page_table, lam, q, _spread_rows(q, place), kn, vn, g, *([cache_kT] * G), *([cache_v] * G))


def _rows_time_major(ref, scr, tc, nb, offset=0):
    for t in range(tc):
        scr[offset + t * nb:offset + (t + 1) * nb, :] = ref[:, t, :]


def _s5_body(u_ref, h0r_ref, h0i_ref, ar_ref, ai_ref, bm_ref, cr_ref, ci_ref, d_ref, wg_ref,
             o_ref, hr_out, hi_out, u_scr, br_scr, bi_scr, hr_scr, hi_scr, *, tc, nb):
    i = pl.program_id(0)
    LB = S5_LANES // S5_BLOCKS
    UB = GROUP_WIDTH // S5_BLOCKS

    @pl.when(i == 0)
    def _():
        hr_scr[...] = h0r_ref[...]
        hi_scr[...] = h0i_ref[...]

    _rows_time_major(u_ref, u_scr, tc, nb)
    u = u_scr[...]
    ys = []
    for b in range(S5_BLOCKS):
        sl = slice(b * LB, (b + 1) * LB)
        bb = jnp.dot(u[:, b * UB:(b + 1) * UB].astype(BF), bm_ref[b], preferred_element_type=F32)
        br_scr[:, sl] = bb[:, :LB]
        bi_scr[:, sl] = bb[:, LB:]
        ar = jnp.broadcast_to(ar_ref[:, sl], (nb, LB))
        ai = jnp.broadcast_to(ai_ref[:, sl], (nb, LB))

        def step(t, h, sl=sl, ar=ar, ai=ai):
            hr, hi = h
            r0 = pl.multiple_of(t * nb, nb)
            nr = ar * hr - ai * hi + br_scr[pl.ds(r0, nb), sl]
            ni = ar * hi + ai * hr + bi_scr[pl.ds(r0, nb), sl]
            br_scr[pl.ds(r0, nb), sl] = nr
            bi_scr[pl.ds(r0, nb), sl] = ni
            return nr, ni

        hr, hi = lax.fori_loop(0, tc, step, (hr_scr[:, sl], hi_scr[:, sl]))
        hr_scr[:, sl] = hr
        hi_scr[:, sl] = hi
        ys.append(jnp.dot(br_scr[:, sl].astype(BF), cr_ref[b], preferred_element_type=F32)
                  + jnp.dot(bi_scr[:, sl].astype(BF), ci_ref[b], preferred_element_type=F32))
    y = jnp.concatenate(ys, axis=-1) + d_ref[...] * u
    g = jax.nn.gelu(y)
    out = g * jax.nn.sigmoid(jnp.dot(g.astype(BF), wg_ref[...], preferred_element_type=F32))
    for t in range(tc):
        o_ref[:, t, :] = out[t * nb:(t + 1) * nb, :]

    @pl.when(i == pl.num_programs(0) - 1)
    def _():
        hr_out[...] = hr_scr[...]
        hi_out[...] = hi_scr[...]


def _s5(u, h0r, h0i, p):
    nb, T, W = u.shape
    tc = min(T, 64)
    assert T % tc == 0
    rows = tc * nb
    st = jax.ShapeDtypeStruct((nb, S5_LANES), F32)
    seq_spec = pl.BlockSpec((nb, tc, W), lambda i: (0, i, 0))
    return pl.pallas_call(
        functools.partial(_s5_body, tc=tc, nb=nb),
        out_shape=[jax.ShapeDtypeStruct((nb, T, W), F32), st, st],
        grid=(T // tc,),
        in_specs=[seq_spec,
                  _full((nb, S5_LANES)), _full((nb, S5_LANES)),
                  _full((1, S5_LANES)), _full((1, S5_LANES)),
                  _full(p["bm"].shape), _full(p["cr"].shape), _full(p["ci"].shape),
                  _full((1, W)), _full(p["wglu"].shape)],
        out_specs=[seq_spec,
                   pl.BlockSpec((nb, S5_LANES), lambda i: (0, 0)),
                   pl.BlockSpec((nb, S5_LANES), lambda i: (0, 0))],
        scratch_shapes=[pltpu.VMEM((rows, W), F32),
                        pltpu.VMEM((rows, S5_LANES), F32), pltpu.VMEM((rows, S5_LANES), F32),
                        pltpu.VMEM((nb, S5_LANES), F32), pltpu.VMEM((nb, S5_LANES), F32)],
        compiler_params=_cp(1),
        name="s5",
    )(u, h0r, h0i, p["ar"], p["ai"], p["bm"], p["cr"], p["ci"], p["d"], p["wglu"])


def _lru_body(xb_ref, gate_ref, cv0_ref, h0_ref, cw_ref, cb_ref, wa_ref, ba_ref, wx_ref, bx_ref, nsp_ref,
              o_ref, cv_out, h_out, xp_scr, a_scr, b_scr, h_scr, *, tc, nb):
    i = pl.program_id(0)
    rows = tc * nb
    tail = (CONV_W - 1) * nb

    @pl.when(i == 0)
    def _():
        xp_scr[0:tail, :] = cv0_ref[...]
        h_scr[...] = h0_ref[...]

    @pl.when(i > 0)
    def _():
        xp_scr[0:tail, :] = xp_scr[rows:rows + tail, :]

    _rows_time_major(xb_ref, xp_scr, tc, nb, offset=tail)
    xc = cb_ref[...] + cw_ref[0:1, :] * xp_scr[0:rows, :]
    for j in range(1, CONV_W):
        xc = xc + cw_ref[j:j + 1, :] * xp_scr[j * nb:j * nb + rows, :]
    xcb = xc.astype(BF)
    r = jax.nn.sigmoid(jnp.dot(xcb, wa_ref[...], preferred_element_type=F32) + ba_ref[...])
    ig = jax.nn.sigmoid(jnp.dot(xcb, wx_ref[...], preferred_element_type=F32) + bx_ref[...])
    log_a = nsp_ref[...] * r
    a_scr[...] = jnp.exp(log_a)
    th = jnp.tanh(log_a)
    b_scr[...] = jnp.sqrt(-2.0 * th / (1.0 - th)) * (ig * xc)

    def step(t, h):
        r0 = pl.multiple_of(t * nb, nb)
        h = a_scr[pl.ds(r0, nb), :] * h + b_scr[pl.ds(r0, nb), :]
        b_scr[pl.ds(r0, nb), :] = h
        return h

    h = lax.fori_loop(0, tc, step, h_scr[...])
    h_scr[...] = h
    for t in range(tc):
        o_ref[:, t, :] = jax.nn.gelu(gate_ref[:, t, :]) * b_scr[t * nb:(t + 1) * nb, :]

    @pl.when(i == pl.num_programs(0) - 1)
    def _():
        cv_out[...] = xp_scr[rows:rows + tail, :]
        h_out[...] = h


def _lru(xb, gate, cv0, h0, p):
    nb, T, W = xb.shape
    tc = min(T, 128)
    assert T % tc == 0
    rows = tc * nb
    tail = (CONV_W - 1) * nb
    row_spec = pl.BlockSpec((nb, tc, W), lambda i: (0, i, 0))
    vec = _full((1, W))
    return pl.pallas_call(
        functools.partial(_lru_body, tc=tc, nb=nb),
        out_shape=[jax.ShapeDtypeStruct((nb, T, W), F32), jax.ShapeDtypeStruct((tail, W), F32),
                   jax.ShapeDtypeStruct((nb, W), F32)],
        grid=(T // tc,),
        in_specs=[row_spec, row_spec, _full((tail, W)), _full((nb, W)), _full((CONV_W, W)), vec,
                  _full((W, W)), vec, _full((W, W)), vec, vec],
        out_specs=[row_spec, pl.BlockSpec((tail, W), lambda i: (0, 0)), pl.BlockSpec((nb, W), lambda i: (0, 0))],
        scratch_shapes=[pltpu.VMEM((rows + tail, W), F32), pltpu.VMEM((rows, W), F32),
                        pltpu.VMEM((rows, W), F32), pltpu.VMEM((nb, W), F32)],
        compiler_params=_cp(1),
        name="rglru",
    )(xb, gate, cv0, h0, p["cw"], p["cb"], p["wa"], p["ba"], p["wx"], p["bx"], p["nsp"])


def _prep_ffn(wg, wu, wd):
    d, f = wg.shape
    return wg.astype(BF), wu.astype(BF), wd.astype(BF).reshape(f // FF_CHUNK, FF_CHUNK, d)


def _prep_s5(a_re, a_im, b_re, b_im, c_re, c_im, d_skip, log_dt, w_glu):
    dt = jnp.exp(log_dt)[:, None]
    mag = jnp.exp(a_re * dt)
    ab_re, ab_im = mag * jnp.cos(a_im * dt), mag * jnp.sin(a_im * dt)
    den = a_re * a_re + a_im * a_im
    nr, ni = ab_re - 1.0, ab_im
    co_re, co_im = (nr * a_re + ni * a_im) / den, (ni * a_re - nr * a_im) / den
    bt_re = co_re[..., None] * b_re - co_im[..., None] * b_im
    bt_im = co_re[..., None] * b_im + co_im[..., None] * b_re
    gpb = S5_GROUPS // S5_BLOCKS
    eye = jnp.eye(gpb, dtype=F32)

    def blk_in(bt):
        x = bt.reshape(S5_BLOCKS, gpb, S5_STATE, S5_CH)
        return jnp.einsum('bgph,gk->bghkp', x, eye).reshape(S5_BLOCKS, gpb * S5_CH, gpb * S5_STATE)

    def blk_out(c):
        x = c.reshape(S5_BLOCKS, gpb, S5_CH, S5_STATE)
        return jnp.einsum('bghp,gk->bgpkh', x, eye).reshape(S5_BLOCKS, gpb * S5_STATE, gpb * S5_CH)

    return {"ar": ab_re.reshape(1, S5_LANES), "ai": ab_im.reshape(1, S5_LANES),
            "bm": jnp.concatenate([blk_in(bt_re), blk_in(bt_im)], axis=-1).astype(BF),
            "cr": blk_out(c_re).astype(BF), "ci": blk_out(-c_im).astype(BF),
            "d": d_skip.reshape(1, GROUP_WIDTH), "wglu": w_glu.astype(BF)}


def _prep_lru(conv_w, conv_b, w_a, b_a, w_x, b_x, lam):
    eye = jnp.eye(LRU_BLOCKS, dtype=F32)
    W = GROUP_WIDTH
    dense = lambda w: jnp.einsum('nde,nm->ndme', w, eye).reshape(W, W).astype(BF)
    row = lambda v: v.reshape(1, W)
    return {"cw": conv_w, "cb": row(conv_b), "wa": dense(w_a), "ba": row(b_a), "wx": dense(w_x),
            "bx": row(b_x), "nsp": row(-LRU_C * jax.nn.softplus(-lam))}


def _heads(a, B, T, width, transposed):
    nh = GROUP_WIDTH // width
    if transposed:
        return a.reshape(B, nh, width, T).transpose(0, 3, 1, 2)
    return a.reshape(B, T, nh, width)


def _trunk(x, B, T, P, st):
    prompt = st is None
    W = GROUP_WIDTH
    kd, vd, sr, si, kf, vf, lf, cv, lr = ([] for _ in range(9))
    depth = len(P["ffn"])
    for l in range(depth):
        g = P["norm_g"][l]
        x = _ffn(x, g[0], P["ffn"][l][0])
        if l % 2 == 0:
            e = l // 2
            lam_init = 0.8 - 0.6 * math.exp(-0.3 * l)
            lam = P["lam"][e]
            res = _inproj_even(x, g[1], P["w_in_even"][e], prompt, T)
            u, q, k, v = res[:4]
            if prompt:
                kb, vTb = res[4:]
                attn = _attn_prompt(q, kb.reshape(B, T, W), vTb, lam, P["sel"], P["subln_g"][e],
                                    fox=False, out_scale=1.0 - lam_init).reshape(B * T, W)
                h0r = h0i = jnp.zeros((B, S5_LANES), F32)
            else:
                nh = W // (2 * DH)
                mh = lambda a: a.reshape(B, nh, 2, DH).transpose(0, 2, 1, 3).reshape(B, 2 * nh, DH)
                vn = v.reshape(B, nh, 2 * DH)
                attn = _diff_decode(st["page_table"], e, lam, mh(q), mh(k),
                                    jnp.concatenate([vn, vn], axis=1), P["subln_g"][e],
                                    st["k_diff"], st["v_diff"], 1.0 - lam_init).reshape(B, W)
                h0r = st["s5_re"][e].reshape(B, S5_LANES)
                h0i = st["s5_im"][e].reshape(B, S5_LANES)
            rec, hr, hi = _s5(u.reshape(B, T, W), h0r, h0i, P["s5"][e])
            rec = rec.reshape(B * T, W)
            mix = (rec, attn) + P["w_out_even"][e]
            kd.append(_heads(k, B, T, DH, transposed=prompt))
            vd.append(_heads(v, B, T, 2 * DH, transposed=False))
            sr.append(hr.reshape(B, S5_GROUPS, S5_STATE))
            si.append(hi.reshape(B, S5_GROUPS, S5_STATE))
        else:
            o = l // 2
            H = N_FOX_HEADS
            res = _inproj_odd(x, g[1], P["w_in_odd"][o], P["w_f"][o], P["b_f"][o], prompt, T)
            xb, gate, q, k, v, logf = res[:6]
            if prompt:
                kb, vTb = res[6:8]
                pieces = jnp.stack(res[8:11], axis=-1)
                ck = jnp.pad(pieces.reshape(B, T, H // 2, 6), ((0, 0), (0, 0), (0, 0), (0, 10)))
                k_aug = jnp.concatenate([kb.reshape(B, T, H // 2, 2 * DH), ck], axis=-1).transpose(0, 2, 1, 3)
                attn = _attn_prompt(q, k_aug, vTb, P["lam"][0], P["sel"], P["subln_g"][0],
                                    fox=True, out_scale=1.0).reshape(B * T, W)
                cv0 = jnp.zeros(((CONV_W - 1) * B, W), F32)
                h0 = jnp.zeros((B, W), F32)
            else:
                hd = lambda a: a.reshape(B, H, DH)
                attn = _fox_decode(st["page_table"], o, hd(q), hd(k), hd(v), logf.reshape(B, H, 1),
                                   st["k_fox"], st["v_fox"], st["logfT"]).reshape(B, W)
                cv0 = st["conv"][o].transpose(1, 0, 2).reshape((CONV_W - 1) * B, W)
                h0 = st["lru"][o]
            rec, cv_new, h_new = _lru(xb.reshape(B, T, W), gate.reshape(B, T, W), cv0, h0, P["lru"][o])
            rec = rec.reshape(B * T, W)
            mix = (rec, attn) + P["w_out_odd"][o]
            kf.append(_heads(k, B, T, DH, transposed=prompt))
            vf.append(_heads(v, B, T, DH, transposed=prompt))
            lf.append(logf.reshape(B, T, H))
            cv.append(cv_new.reshape(CONV_W - 1, B, W).transpose(1, 0, 2))
            lr.append(h_new)
        x = _ffn(x, g[2], P["ffn"][l][1], final_g=P["final_norm_g"] if l == depth - 1 else None, mix=mix)
    return (x.reshape(B, T, D_MODEL), jnp.stack(kd), jnp.stack(vd), jnp.stack(sr), jnp.stack(si),
            jnp.stack(kf), jnp.stack(vf), jnp.stack(lf), jnp.stack(cv), jnp.stack(lr))


def kernel(x_prompt, x_sample, cache_k_diff, cache_v_diff, state_s5_re, state_s5_im, cache_k_fox, cache_v_fox, cache_logf_fox, state_conv, state_lru, page_table, norm_g, final_norm_g, w_ffn_gate, w_ffn_up, w_ffn_down, w_in_even, w_out_even, s5_a_re, s5_a_im, s5_b_re, s5_b_im, s5_c_re, s5_c_im, s5_d, s5_log_dt, s5_w_glu, lambda_q1, lambda_k1, lambda_q2, lambda_k2, diff_subln_g, w_in_odd, w_out_odd, fox_b_f, conv_w, conv_b, lru_w_a, lru_b_a, lru_w_x, lru_b_x, lru_lambda):
    depth = w_ffn_gate.shape[0]
    n_even, n_odd = w_in_even.shape[0], w_in_odd.shape[0]
    W = GROUP_WIDTH
    H = N_FOX_HEADS
    main = 2 * W + 3 * W
    lam = (jnp.exp(jnp.sum(lambda_q1 * lambda_k1, axis=-1)) - jnp.exp(jnp.sum(lambda_q2 * lambda_k2, axis=-1)))
    lam_init = jnp.array([0.8 - 0.6 * math.exp(-0.3 * 2 * e) for e in range(n_even)], F32)
    P = {
        "norm_g": norm_g, "final_norm_g": final_norm_g,
        "ffn": [[_prep_ffn(w_ffn_gate[l, i], w_ffn_up[l, i], w_ffn_down[l, i]) for i in range(2)]
                for l in range(depth)],
        "w_in_even": w_in_even.astype(BF),
        "w_out_even": [(w_out_even[e, :W].astype(BF), w_out_even[e, W:].astype(BF)) for e in range(n_even)],
        "s5": [_prep_s5(s5_a_re[e], s5_a_im[e], s5_b_re[e], s5_b_im[e], s5_c_re[e], s5_c_im[e],
                        s5_d[e], s5_log_dt[e], s5_w_glu[e]) for e in range(n_even)],
        "lam": (lam + lam_init).reshape(n_even, 1),
        "subln_g": diff_subln_g.reshape(n_even, 1, 2 * DH),
        "w_in_odd": w_in_odd[:, :, :main].astype(BF),
        "w_f": jnp.pad(w_in_odd[:, :, main:], ((0, 0), (0, 0), (0, 128 - H))).astype(BF),
        "b_f": jnp.pad(fox_b_f, ((0, 0), (0, 128 - H))).reshape(n_odd, 1, 128),
        "w_out_odd": [(w_out_odd[o, :W].astype(BF), w_out_odd[o, W:].astype(BF)) for o in range(n_odd)],
        "lru": [_prep_lru(conv_w[o], conv_b[o], lru_w_a[o], lru_b_a[o], lru_w_x[o], lru_b_x[o], lru_lambda[o])
                for o in range(n_odd)],
        "sel": (lax.broadcasted_iota(jnp.int32, (16, 2 * ATTN_QB), 0) // 3
                == lax.broadcasted_iota(jnp.int32, (16, 2 * ATTN_QB), 1) // ATTN_QB).astype(BF),
    }
    Bp, Tp, _ = x_prompt.shape
    Bs, Ts, _ = x_sample.shape
    n_phys, page = cache_k_diff.shape[1], cache_k_diff.shape[2]
    st = {
        "page_table": page_table,
        "k_diff": cache_k_diff.transpose(0, 1, 3, 4, 2),
        "v_diff": cache_v_diff.reshape(n_even, n_phys, page * (W // (2 * DH)), 2 * DH),
        "k_fox": cache_k_fox.transpose(0, 1, 3, 4, 2),
        "v_fox": cache_v_fox.transpose(0, 1, 3, 4, 2),
        "logfT": cache_logf_fox.transpose(0, 1, 3, 2),
        "s5_re": state_s5_re, "s5_im": state_s5_im, "conv": state_conv, "lru": state_lru,
    }
    outs_p = _trunk(x_prompt.reshape(Bp * Tp, D_MODEL), Bp, Tp, P, None)
    outs_s = _trunk(x_sample.reshape(Bs * Ts, D_MODEL), Bs, Ts, P, st)
    return (outs_p[0], outs_s[0]) + tuple(outs_p[1:]) + tuple(outs_s[1:])
```
